```python
import jax, jax.numpy as jnp
from jax import lax
import numpy as np

D_MODEL = 2048
BATCH = 4
SEQ = 4096
DEPTH = 4

GRID_W = 64
CTX_LEN = 256
HEAD_DIM = 64
N_BRANCHES = 4
BRANCH_W = D_MODEL // N_BRANCHES
FOURIER_GROUPS = 4
NA_HEADS = BRANCH_W // HEAD_DIM
NA_WIN_ROWS = 8
NA_WIN_COLS = 16
ROPE_THETA = 10000.0
RW_HEADS = BRANCH_W // HEAD_DIM
RW_DECAY_LORA = 64
RW_ICL_LORA = 64
RW_GATE_LORA = 128
RW_GN_EPS = 64e-5
ML_HEADS = BRANCH_W // HEAD_DIM
ML_CHUNK = 64
ML_CONV = 3
N_EXPERTS = 32
TOP_K = 4
D_EXPERT = D_MODEL // 4
SWIGLU_LIMIT = 7.0
SWIGLU_ALPHA = 1.702
MOE_BLOCK = 256
NORM_EPS = 1e-6
GN_EPS = 1e-5

STATE_SIZES = (BRANCH_W, BRANCH_W,
               BRANCH_W, BRANCH_W, 2 * RW_DECAY_LORA, 2 * RW_ICL_LORA,
               BRANCH_W, BRANCH_W, 4 * ML_HEADS)
OUT_SIZES = (BRANCH_W, BRANCH_W, BRANCH_W, RW_GATE_LORA, BRANCH_W, BRANCH_W, N_BRANCHES * D_MODEL)
STATE_W = 6 * BRANCH_W + 2 * RW_DECAY_LORA + 2 * RW_ICL_LORA + 4 * ML_HEADS
IN_W = STATE_W + 5 * BRANCH_W + RW_GATE_LORA + N_BRANCHES * D_MODEL

kernel_name = 'hybrid_flow_parallel_mixers_moe'


def _split(p, sizes):
    out, o = [], 0
    for s in sizes:
        out.append(p[..., o:o + s])
        o += s
    return out


def _heads(t):
    return t.reshape(t.shape[:2] + (-1, HEAD_DIM))


def _rms_norm(x, g):
    xf = x.astype(jnp.float32)
    y = xf * lax.rsqrt(jnp.mean(xf * xf, -1, keepdims=True) + NORM_EPS)
    return (y * g.astype(jnp.float32)).astype(x.dtype)


def _group_norm(y, g, b, n_heads, eps):
    y = y.astype(jnp.float32)
    shp = y.shape
    yh = y.reshape(shp[:-1] + (n_heads, shp[-1] // n_heads))
    mu = jnp.mean(yh, -1, keepdims=True)
    var = jnp.mean(jnp.square(yh - mu), -1, keepdims=True)
    out = ((yh - mu) * lax.rsqrt(var + eps)).reshape(shp) * g
    return out if b is None else out + b


def _fourier(u):
    B, L, _ = u.shape
    ug = u.astype(jnp.float32).reshape(B, L, FOURIER_GROUPS, BRANCH_W // FOURIER_GROUPS)
    y = jnp.fft.fft2(ug, axes=(1, 3), norm='ortho').real
    return y.reshape(B, L, BRANCH_W).astype(u.dtype)


def _axial_rope(t, rows, cols):
    half = HEAD_DIM // 2
    nf = half // 2
    inv = ROPE_THETA ** (-jnp.arange(nf, dtype=jnp.float32) / nf)
    tf = t.astype(jnp.float32)

    def rot(u, pos):
        ang = pos.astype(jnp.float32)[:, None] * inv
        cos = jnp.cos(ang)[None, :, None, :]
        sin = jnp.sin(ang)[None, :, None, :]
        u1, u2 = u[..., :nf], u[..., nf:]
        return jnp.concatenate([u1 * cos - u2 * sin, u1 * sin + u2 * cos], -1)

    return jnp.concatenate([rot(tf[..., :half], rows), rot(tf[..., half:], cols)], -1).astype(t.dtype)


def _na_latent(q, k, v, kc, vc, rpb):
    B, S, H, dh = q.shape
    n_rows = S // GRID_W
    win_r = min(NA_WIN_ROWS, n_rows)
    scale = dh ** -0.5
    t = jnp.arange(S)
    q_rot = _axial_rope(q, t // GRID_W, t % GRID_W).reshape(B, n_rows, GRID_W, H, dh)
    k_rot = _axial_rope(k, t // GRID_W, t % GRID_W).reshape(B, n_rows, GRID_W, H, dh)
    v_g = v.reshape(B, n_rows, GRID_W, H, dh)
    r = jnp.arange(n_rows)
    w = jnp.arange(GRID_W)
    row_idx = jnp.clip(r - win_r // 2, 0, n_rows - win_r)[:, None] + jnp.arange(win_r)[None, :]
    col_start = jnp.clip(w - NA_WIN_COLS // 2, 0, GRID_W - NA_WIN_COLS)
    col_in = (w[None, :] >= col_start[:, None]) & (w[None, :] < col_start[:, None] + NA_WIN_COLS)
    k_rows = k_rot[:, row_idx]
    v_rows = v_g[:, row_idx]
    s_loc = jnp.einsum('brqhd,brawhd->bhrqaw', q_rot, k_rows, preferred_element_type=jnp.float32) * scale
    row_off = row_idx - r[:, None] + NA_WIN_ROWS - 1
    col_off = jnp.clip(w[None, :] - w[:, None], 1 - NA_WIN_COLS, NA_WIN_COLS - 1) + NA_WIN_COLS - 1
    bias = rpb.astype(jnp.float32)[:, row_off[:, None, :, None], col_off[None, :, None, :]]
    s_loc = jnp.where(col_in[:, None, :], s_loc + bias, -jnp.inf)
    q_plain = q.reshape(B, n_rows, GRID_W, H, dh)
    s_ctx = jnp.einsum('brqhd,bchd->bhrqc', q_plain, kc, preferred_element_type=jnp.float32) * scale
    n_loc = win_r * GRID_W
    p = jax.nn.softmax(jnp.concatenate([s_loc.reshape(B, H, n_rows, GRID_W, n_loc), s_ctx], -1), axis=-1)
    p_loc = p[..., :n_loc].reshape(B, H, n_rows, GRID_W, win_r, GRID_W).astype(v.dtype)
    p_ctx = p[..., n_loc:].astype(v.dtype)
    o = jnp.einsum('bhrqaw,brawhd->brqhd', p_loc, v_rows) + jnp.einsum('bhrqc,bchd->brqhd', p_ctx, vc)
    return o.reshape(B, S, H * dh)


def _attn_ctx(qc, kc, vc):
    s = jnp.einsum('bqhd,bkhd->bhqk', qc, kc, preferred_element_type=jnp.float32) * HEAD_DIM ** -0.5
    p = jax.nn.softmax(s, axis=-1).astype(vc.dtype)
    o = jnp.einsum('bhqk,bkhd->bqhd', p, vc)
    return o.reshape(o.shape[0], o.shape[1], -1)


def _shift(u, direction):
    if direction == 0:
        return jnp.pad(u[:, :-1], ((0, 0), (1, 0), (0, 0)))
    return jnp.pad(u[:, 1:], ((0, 0), (0, 1), (0, 0)))


def _lerp_shift(u, mu, direction):
    return u + mu * (_shift(u, direction) - u)


def _rwkv_direction(r, k, v, wl, al, d, lp, state, need_out):
    f32 = jnp.float32
    B, L, _ = k.shape
    hs = (B, L, RW_HEADS, HEAD_DIM)
    mu = lp['rw_mu_rkv'][d]
    mu_wa = lp['rw_mu_wa'][d]
    k = _lerp_shift(k, mu[BRANCH_W:2 * BRANCH_W], d).astype(f32)
    v = _lerp_shift(v, mu[2 * BRANCH_W:], d).astype(f32)
    wl = _lerp_shift(wl, mu_wa[:RW_DECAY_LORA], d)
    al = _lerp_shift(al, mu_wa[RW_DECAY_LORA:], d)
    w_log = -jax.nn.softplus(-(lp['rw_w0'][d] + jnp.tanh(wl) @ lp['rw_w_up'][d]).astype(f32)) - 0.5
    decay = jnp.exp(-jnp.exp(w_log)).reshape(hs)
    a = jax.nn.sigmoid((lp['rw_a0'][d] + al @ lp['rw_a_up'][d]).astype(f32))
    kk = (k * lp['rw_k_k']).reshape(hs)
    kk = kk / jnp.maximum(jnp.sqrt(jnp.sum(kk * kk, -1, keepdims=True)), 1e-12)
    k_mod = (k * (1.0 + (a - 1.0) * lp['rw_k_a'])).reshape(hs)
    a = a.reshape(hs)
    v = v.reshape(hs)
    xs = (decay, k_mod, v, -kk, kk * a)
    if need_out:
        r = _lerp_shift(r, mu[:BRANCH_W], d).astype(f32).reshape(hs)
        xs = xs + (r,)

    def step(S, inp):
        w_t, k_t, v_t, a_t, b_t = inp[:5]
        sa = jnp.einsum('bhvk,bhk->bhv', S, a_t)
        S = S * w_t[:, :, None, :] + sa[..., None] * b_t[:, :, None, :] + v_t[..., None] * k_t[:, :, None, :]
        if need_out:
            return S, jnp.einsum('bhvk,bhk->bhv', S, inp[5])
        return S, None

    S_fin, ys = lax.scan(step, state, tuple(jnp.moveaxis(t, 1, 0) for t in xs), reverse=(d == 1))
    if not need_out:
        return S_fin, None
    y = _group_norm(jnp.moveaxis(ys, 0, 1).reshape(B, L, BRANCH_W), lp['rw_gn_g'][d], lp['rw_gn_b'][d],
                    RW_HEADS, RW_GN_EPS)
    bonus = jnp.sum(r * k_mod * lp['rw_r_k'][d], -1, keepdims=True) * v
    return S_fin, y + bonus.reshape(B, L, BRANCH_W)


def _dw_conv(u, w):
    return lax.conv_general_dilated(u, w[:, None, :].astype(u.dtype), window_strides=(1,),
                                    padding=[(ML_CONV // 2, ML_CONV // 2)],
                                    dimension_numbers=('NWC', 'WIO', 'NWC'),
                                    feature_group_count=u.shape[-1])


def _mlstm_inputs(q, k, v, g, lp):
    f32 = jnp.float32
    B, L, _ = k.shape
    hs = (B, L, ML_HEADS, HEAD_DIM)
    k = (jax.nn.silu(_dw_conv(k, lp['ml_conv_k'])).astype(f32) * HEAD_DIM ** -0.5).reshape(hs)
    v = v.astype(f32).reshape(hs)
    if q is not None:
        q = jax.nn.silu(_dw_conv(q, lp['ml_conv_q'])).astype(f32).reshape(hs)
    g = g.astype(f32).reshape(B, L, 4, ML_HEADS) + lp['ml_gate_b'].astype(f32)
    return q, k, v, g[:, :, :2], jax.nn.log_sigmoid(g[:, :, 2:])


def _mlstm_chunkwise(q, k, v, li, lf, state, need_out):
    B, L, H, dh = k.shape
    nc = L // ML_CHUNK

    def chunks(t):
        t = t.reshape((B, nc, ML_CHUNK) + t.shape[2:])
        return jnp.swapaxes(jnp.moveaxis(t, 1, 0), 2, 3)

    lower = jnp.tril(jnp.ones((ML_CHUNK, ML_CHUNK), dtype=bool))

    def step(carry, xs):
        C, n, m = carry
        kc, vc, lic, lfc = xs[:4]
        b = jnp.cumsum(lfc, axis=-1)
        b_end = b[..., -1]
        log_in = b_end[..., None] - b + lic
        m_new = jnp.maximum(b_end + m, jnp.max(log_in, -1))
        carry_scale = jnp.exp(b_end + m - m_new)
        w_s = jnp.exp(log_in - m_new[..., None])
        C_new = carry_scale[..., None, None] * C + jnp.einsum('bhs,bhsv,bhsk->bhvk', w_s, vc, kc)
        n_new = carry_scale[..., None] * n + jnp.einsum('bhs,bhsk->bhk', w_s, kc)
        if not need_out:
            return (C_new, n_new, m_new), None
        qc = xs[4]
        g = b + m[..., None]
        log_d = jnp.where(lower, b[..., :, None] - b[..., None, :] + lic[..., None, :], -jnp.inf)
        m_t = jnp.maximum(g, jnp.max(log_d, -1))
        inter = jnp.exp(g - m_t)
        qk = jnp.einsum('bhtd,bhsd->bhts', qc, kc) * jnp.exp(log_d - m_t[..., None])
        num = inter[..., None] * jnp.einsum('bhvk,bhtk->bhtv', C, qc) + jnp.einsum('bhts,bhsv->bhtv', qk, vc)
        den = inter * jnp.einsum('bhk,bhtk->bht', n, qc) + jnp.sum(qk, -1)
        h = num / jnp.maximum(jnp.abs(den), jnp.exp(-m_t))[..., None]
        return (C_new, n_new, m_new), h

    xs = (chunks(k), chunks(v), chunks(li), chunks(lf)) + ((chunks(q),) if need_out else ())
    st, hs = lax.scan(step, state, xs)
    if not need_out:
        return st, None
    h = jnp.moveaxis(jnp.swapaxes(hs, 2, 3), 0, 1).reshape(B, L, H, dh)
    return st, h


def _mlstm_direction(q, k, v, li, lf, d, state, need_out):
    rev = (lambda t: jnp.flip(t, axis=1)) if d == 1 else (lambda t: t)
    st, h = _mlstm_chunkwise(rev(q) if need_out else None, rev(k), rev(v), rev(li), rev(lf), state, need_out)
    return st, (rev(h) if need_out else None)


def _merge(branches, gate_pre, w_branch, w_out):
    acc = 0.0
    for i in range(N_BRANCHES):
        g = jax.nn.sigmoid(gate_pre[..., i * D_MODEL:(i + 1) * D_MODEL])
        acc = acc + g * (branches[i].astype(gate_pre.dtype) @ w_branch[i])
    return acc @ w_out


def _mixer(hx, hc, lp, last):
    f32 = jnp.float32
    need_c = not last
    B, S, _ = hx.shape
    w_in = lp['w_in']
    px = hx @ w_in
    pc = hc @ (w_in if need_c else w_in[:, :STATE_W])
    na_k, na_v, rw_k, rw_v, rw_wl, rw_al, ml_k, ml_v, ml_g = _split(px[..., :STATE_W], STATE_SIZES)
    fo, na_q, rw_r, rw_gl, ml_q, ml_o, gates = _split(px[..., STATE_W:], OUT_SIZES)
    na_kc, na_vc, rw_kc, rw_vc, rw_wlc, rw_alc, ml_kc, ml_vc, ml_gc = _split(pc[..., :STATE_W], STATE_SIZES)
    if need_c:
        fo_c, na_qc, rw_rc, rw_glc, ml_qc, ml_oc, gates_c = _split(pc[..., STATE_W:], OUT_SIZES)
    else:
        rw_rc = None
        ml_qc = None

    y_four = _fourier(fo)

    kc_h, vc_h = _heads(na_kc), _heads(na_vc)
    y_na = _na_latent(_heads(na_q), _heads(na_k), _heads(na_v), kc_h, vc_h, lp['na_rpb'])

    zero_S = jnp.zeros((B, RW_HEADS, HEAD_DIM, HEAD_DIM), f32)
    y_rw, y_rw_c = 0.0, 0.0
    for d in (0, 1):
        dl = slice(d * RW_DECAY_LORA, (d + 1) * RW_DECAY_LORA)
        da = slice(d * RW_ICL_LORA, (d + 1) * RW_ICL_LORA)
        S_c, o_c = _rwkv_direction(rw_rc, rw_kc, rw_vc, rw_wlc[..., dl], rw_alc[..., da], d, lp, zero_S, need_c)
        _, o_x = _rwkv_direction(rw_r, rw_k, rw_v, rw_wl[..., dl], rw_al[..., da], d, lp, S_c, True)
        y_rw = y_rw + o_x
        if need_c:
            y_rw_c = y_rw_c + o_c
    y_rw = y_rw * (jax.nn.sigmoid(rw_gl) @ lp['rw_g_up'])

    ml_state0 = (jnp.zeros((B, ML_HEADS, HEAD_DIM, HEAD_DIM), f32), jnp.zeros((B, ML_HEADS, HEAD_DIM), f32),
                 jnp.zeros((B, ML_HEADS), f32))
    qx, kx, vx, lix, lfx = _mlstm_inputs(ml_q, ml_k, ml_v, ml_g, lp)
    qcx, kcx, vcx, lic, lfc = _mlstm_inputs(ml_qc, ml_kc, ml_vc, ml_gc, lp)
    h_x, h_c = 0.0, 0.0
    for d in (0, 1):
        st_c, hc_d = _mlstm_direction(qcx, kcx, vcx, lic[:, :, d], lfc[:, :, d], d, ml_state0, need_c)
        _, hx_d = _mlstm_direction(qx, kx, vx, lix[:, :, d], lfx[:, :, d], d, st_c, True)
        h_x = h_x + hx_d
        if need_c:
            h_c = h_c + hc_d
    y_ml = _group_norm(h_x.reshape(B, S, BRANCH_W), lp['ml_gn_g'], None, ML_HEADS, GN_EPS) * jax.nn.sigmoid(ml_o)

    yx = _merge([y_four, y_na, y_rw, y_ml], gates, lp['w_branch'], lp['w_out'])
    if last:
        return yx, None

    Lc = hc.shape[1]
    y_four_c = _fourier(fo_c)
    y_na_c = _attn_ctx(_heads(na_qc), kc_h, vc_h)
    y_rw_c = y_rw_c * (jax.nn.sigmoid(rw_glc) @ lp['rw_g_up'])
    y_ml_c = _group_norm(h_c.reshape(B, Lc, BRANCH_W), lp['ml_gn_g'], None, ML_HEADS, GN_EPS) * jax.nn.sigmoid(ml_oc)
    yc = _merge([y_four_c, y_na_c, y_rw_c, y_ml_c], gates_c, lp['w_branch'], lp['w_out'])
    return yx, yc


def _moe(h, w_router, b_router, w_gu, b_gu, w_dn, b_dn):
    T, D = h.shape
    f32 = jnp.float32
    logits = (h @ w_router).astype(f32) + b_router.astype(f32)
    top_val, top_idx = lax.top_k(logits, TOP_K)
    gate = jax.nn.softmax(top_val, axis=-1)
    flat_e = top_idx.reshape(-1)
    flat_tok = jnp.repeat(jnp.arange(T, dtype=jnp.int32), TOP_K)
    flat_g = gate.reshape(-1)
    order = jnp.argsort(flat_e)
    e_sorted = flat_e[order]
    counts = jnp.bincount(flat_e, length=N_EXPERTS)
    padded = (counts + MOE_BLOCK - 1) // MOE_BLOCK * MOE_BLOCK
    pad_end = jnp.cumsum(padded)
    pad_start = pad_end - padded
    start = jnp.cumsum(counts) - counts
    dest = pad_start[e_sorted] + jnp.arange(T * TOP_K, dtype=jnp.int32) - start[e_sorted]
    n_blocks = (T * TOP_K + MOE_BLOCK - 1) // MOE_BLOCK + N_EXPERTS
    tok_buf = jnp.full((n_blocks * MOE_BLOCK,), T, jnp.int32).at[dest].set(flat_tok[order])
    g_buf = jnp.zeros((n_blocks * MOE_BLOCK,), f32).at[dest].set(flat_g[order])
    block_e = jnp.minimum(jnp.searchsorted(pad_end, jnp.arange(n_blocks) * MOE_BLOCK, side='right'), N_EXPERTS - 1)
    h_pad = jnp.concatenate([h, jnp.zeros((1, D), h.dtype)], axis=0)

    def expert_block(args):
        tok, e = args
        gu = h_pad[tok] @ w_gu[e] + b_gu[e]
        gt = jnp.minimum(gu[:, :D_EXPERT], SWIGLU_LIMIT)
        up = jnp.clip(gu[:, D_EXPERT:], -SWIGLU_LIMIT, SWIGLU_LIMIT)
        return ((up + 1.0) * (gt * jax.nn.sigmoid(SWIGLU_ALPHA * gt))) @ w_dn[e] + b_dn[e]

    ys = lax.map(expert_block, (tok_buf.reshape(n_blocks, MOE_BLOCK), block_e))
    out = jnp.zeros((T + 1, D), f32).at[tok_buf].add(ys.reshape(-1, D).astype(f32) * g_buf[:, None])
    return out[:T].astype(h.dtype)


def setup_inputs(seed: int = 0) -> dict:
    key = jax.random.key(seed)
    ks = iter(jax.random.split(key, 48))
    f32 = jnp.float32

    def nrm(shape, scale):
        return jax.random.normal(next(ks), shape, f32) * scale

    def uni(shape, lo, hi):
        return jax.random.uniform(next(ks), shape, f32, lo, hi)

    L, D, F, E = DEPTH, D_MODEL, D_EXPERT, N_EXPERTS
    return {
        'x': nrm((BATCH, SEQ, D), 1.0),
        'c': nrm((BATCH, D), 1.0),
        'ctx': nrm((BATCH, CTX_LEN, D), 1.0),
        'c_ctx': nrm((D,), 1.0),
        'w_ada': nrm((L, D, 6 * D), 0.5 * D ** -0.5),
        'b_ada': nrm((L, 6 * D), 0.02),
        'g_norm1': 1.0 + nrm((L, D), 0.02),
        'g_norm2': 1.0 + nrm((L, D), 0.02),
        'w_in': nrm((L, D, IN_W), D ** -0.5),
        'na_rpb': nrm((L, NA_HEADS, 2 * NA_WIN_ROWS - 1, 2 * NA_WIN_COLS - 1), 0.1),
        'rw_mu_rkv': uni((L, 2, 3 * BRANCH_W), 0.2, 0.8),
        'rw_mu_wa': uni((L, 2, RW_DECAY_LORA + RW_ICL_LORA), 0.2, 0.8),
        'rw_w_up': nrm((L, 2, RW_DECAY_LORA, BRANCH_W), 0.1),
        'rw_w0': nrm((L, 2, BRANCH_W), 0.5),
        'rw_a_up': nrm((L, 2, RW_ICL_LORA, BRANCH_W), 0.1),
        'rw_a0': nrm((L, 2, BRANCH_W), 0.1),
        'rw_g_up': nrm((L, RW_GATE_LORA, BRANCH_W), RW_GATE_LORA ** -0.5),
        'rw_k_k': 0.85 + nrm((L, BRANCH_W), 0.05),
        'rw_k_a': 1.0 + nrm((L, BRANCH_W), 0.05),
        'rw_r_k': nrm((L, 2, RW_HEADS, HEAD_DIM), 0.1),
        'rw_gn_g': 1.0 + nrm((L, 2, BRANCH_W), 0.02),
        'rw_gn_b': nrm((L, 2, BRANCH_W), 0.02),
        'ml_conv_q': nrm((L, ML_CONV, BRANCH_W), ML_CONV ** -0.5),
        'ml_conv_k': nrm((L, ML_CONV, BRANCH_W), ML_CONV ** -0.5),
        'ml_gate_b': jnp.concatenate([nrm((L, 2, ML_HEADS), 0.1), uni((L, 2, ML_HEADS), 3.0, 6.0)], axis=1),
        'ml_gn_g': 1.0 + nrm((L, BRANCH_W), 0.02),
        'w_branch': nrm((L, N_BRANCHES, BRANCH_W, D), BRANCH_W ** -0.5),
        'w_out': nrm((L, D, D), D ** -0.5),
        'w_router': nrm((L, D, E), D ** -0.5),
        'b_router': nrm((L, E), 0.01),
        'w_gu': nrm((L, E, D, 2 * F), D ** -0.5),
        'b_gu': nrm((L, E, 2 * F), 0.01),
        'w_dn': nrm((L, E, F, D), F ** -0.5),
        'b_dn': nrm((L, E, D), 0.01),
        'g_final': 1.0 + nrm((D,), 0.02),
    }


def reference(x, c, ctx, c_ctx, w_ada, b_ada, g_norm1, g_norm2, w_in, na_rpb, rw_mu_rkv, rw_mu_wa,
              rw_w_up, rw_w0, rw_a_up, rw_a0, rw_g_up, rw_k_k, rw_k_a, rw_r_k, rw_gn_g, rw_gn_b,
              ml_conv_q, ml_conv_k, ml_gate_b, ml_gn_g, w_branch, w_out, w_router, b_router,
              w_gu, b_gu, w_dn, b_dn, g_final):
    D = D_MODEL
    silu_c = jax.nn.silu(c)
    silu_cc = jax.nn.silu(c_ctx)
    for l in range(DEPTH):
        last = l == DEPTH - 1
        lp = {'w_in': w_in[l], 'na_rpb': na_rpb[l], 'rw_mu_rkv': rw_mu_rkv[l], 'rw_mu_wa': rw_mu_wa[l],
              'rw_w_up': rw_w_up[l], 'rw_w0': rw_w0[l], 'rw_a_up': rw_a_up[l], 'rw_a0': rw_a0[l],
              'rw_g_up': rw_g_up[l], 'rw_k_k': rw_k_k[l], 'rw_k_a': rw_k_a[l], 'rw_r_k': rw_r_k[l],
              'rw_gn_g': rw_gn_g[l], 'rw_gn_b': rw_gn_b[l], 'ml_conv_q': ml_conv_q[l],
              'ml_conv_k': ml_conv_k[l], 'ml_gate_b': ml_gate_b[l], 'ml_gn_g': ml_gn_g[l],
              'w_branch': w_branch[l], 'w_out': w_out[l]}
        mx = jnp.split(silu_c @ w_ada[l] + b_ada[l], 6, axis=-1)
        if last:
            mc = jnp.split(silu_cc @ w_ada[l][:, :2 * D] + b_ada[l][:2 * D], 2, axis=-1)
        else:
            mc = jnp.split(silu_cc @ w_ada[l] + b_ada[l], 6, axis=-1)
        hx = _rms_norm(x, g_norm1[l]) * (1.0 + mx[1][:, None]) + mx[0][:, None]
        hc = _rms_norm(ctx, g_norm1[l]) * (1.0 + mc[1]) + mc[0]
        yx, yc = _mixer(hx, hc, lp, last)
        x = x + mx[2][:, None] * yx
        hx2 = _rms_norm(x, g_norm2[l]) * (1.0 + mx[4][:, None]) + mx[3][:, None]
        if last:
            y = _moe(hx2.reshape(-1, D), w_router[l], b_router[l], w_gu[l], b_gu[l], w_dn[l], b_dn[l])
            x = x + mx[5][:, None] * y.reshape(x.shape)
        else:
            ctx = ctx + mc[2] * yc
            hc2 = _rms_norm(ctx, g_norm2[l]) * (1.0 + mc[4]) + mc[3]
            n_x = x.shape[0] * x.shape[1]
            y = _moe(jnp.concatenate([hx2.reshape(-1, D), hc2.reshape(-1, D)], axis=0),
                     w_router[l], b_router[l], w_gu[l], b_gu[l], w_dn[l], b_dn[l])
            x = x + mx[5][:, None] * y[:n_x].reshape(x.shape)
            ctx = ctx + mc[5] * y[n_x:].reshape(ctx.shape)
    return _rms_norm(x, g_final)
```

```python
import functools
import math

import jax
import jax.numpy as jnp
from jax import lax
from jax.experimental import pallas as pl
from jax.experimental.pallas import tpu as pltpu

f32 = jnp.float32
bf16 = jnp.bfloat16

HEAD_DIM = 64
GRID_W = 64
N_BRANCHES = 4
FOURIER_GROUPS = 4
NA_WIN_ROWS = 8
NA_WIN_COLS = 16
ROPE_THETA = 10000.0
RW_DECAY_LORA = 64
RW_ICL_LORA = 64
RW_GATE_LORA = 128
RW_GN_EPS = 64e-5
ML_CHUNK = 64
RW_CHUNK = 64
TOP_K = 4
SWIGLU_LIMIT = 7.0
SWIGLU_ALPHA = 1.702
MOE_BLOCK = 256
NORM_EPS = 1e-6
GN_EPS = 1e-5
NEG_BIG = -1e30

LANE = 128
VMEM_LIMIT = 56 * 1024 * 1024


def _bdot(a, b):
    return jnp.dot(a.astype(bf16), b.astype(bf16), preferred_element_type=f32)


def _bdot_nt(a, b):
    return lax.dot_general(a.astype(bf16), b.astype(bf16), (((1,), (1,)), ((), ())),
                           preferred_element_type=f32)


def _bdot_tn(a, b):
    return lax.dot_general(a.astype(bf16), b.astype(bf16), (((0,), (0,)), ((), ())),
                           preferred_element_type=f32)


def _split_dot(tri, x):
    hi = x.astype(bf16)
    lo = (x - hi.astype(f32)).astype(bf16)
    return (jnp.dot(tri, hi, preferred_element_type=f32) + jnp.dot(tri, lo, preferred_element_type=f32))


def _split_dot_r(x, tri):
    hi = x.astype(bf16)
    lo = (x - hi.astype(f32)).astype(bf16)
    return (jnp.dot(hi, tri, preferred_element_type=f32) + jnp.dot(lo, tri, preferred_element_type=f32))


def _pick(dim, cap, align):
    if dim <= cap:
        return dim
    t = cap - cap % align
    while t >= align:
        if dim % t == 0:
            return t
        t -= align
    return dim


def _mm_kernel_single(a_ref, b_ref, o_ref):
    o_ref[...] = jnp.dot(a_ref[...], b_ref[...], preferred_element_type=f32).astype(o_ref.dtype)


def _mm_kernel_acc(a_ref, b_ref, o_ref, acc_ref, *, nk):
    k = pl.program_id(2)

    @pl.when(k == 0)
    def _():
        acc_ref[...] = jnp.zeros_like(acc_ref)

    acc_ref[...] += jnp.dot(a_ref[...], b_ref[...], preferred_element_type=f32)

    @pl.when(k == nk - 1)
    def _():
        o_ref[...] = acc_ref[...].astype(o_ref.dtype)


def _mm(a, b, out_dtype=f32):
    a = a.astype(bf16)
    b = b.astype(bf16)
    M, K = a.shape
    N = b.shape[1]
    tm = _pick(M, 1024, 16)
    tn = _pick(N, 1024, LANE)
    tk = _pick(K, 2048, LANE)
    nk = K // tk
    grid = (M // tm, N // tn, nk)
    in_specs = [pl.BlockSpec((tm, tk), lambda i, j, k: (i, k)),
                pl.BlockSpec((tk, tn), lambda i, j, k: (k, j))]
    out_spec = pl.BlockSpec((tm, tn), lambda i, j, k: (i, j))
    params = pltpu.CompilerParams(dimension_semantics=("parallel", "parallel", "arbitrary"),
                                  vmem_limit_bytes=VMEM_LIMIT)
    if nk == 1:
        return pl.pallas_call(_mm_kernel_single, grid=grid, in_specs=in_specs, out_specs=out_spec,
                              out_shape=jax.ShapeDtypeStruct((M, N), out_dtype),
                              compiler_params=params, name="mm")(a, b)
    return pl.pallas_call(functools.partial(_mm_kernel_acc, nk=nk), grid=grid, in_specs=in_specs,
                          out_specs=out_spec, out_shape=jax.ShapeDtypeStruct((M, N), out_dtype),
                          scratch_shapes=[pltpu.VMEM((tm, tn), f32)],
                          compiler_params=params, name="mm_acc")(a, b)


def _merge_kernel(*refs, nb):
    y_refs, g_refs, w_ref, o_ref = refs[:nb], refs[nb:2 * nb], refs[2 * nb], refs[2 * nb + 1]
    acc = None
    for i in range(nb):
        p = jnp.dot(y_refs[i][...], w_ref[i], preferred_element_type=f32)
        t = p / (1.0 + jnp.exp(-g_refs[i][...]))
        acc = t if acc is None else acc + t
    o_ref[...] = acc.astype(o_ref.dtype)


def _merge(ys, gates, w_branch):
    nb = len(ys)
    M, W = ys[0].shape
    D = w_branch.shape[-1]
    tm = _pick(M, 512, 16)
    tn = _pick(D, 1024, LANE)
    nj = D // tn
    y_specs = [pl.BlockSpec((tm, W), lambda i, j: (i, 0)) for _ in range(nb)]
    g_specs = [pl.BlockSpec((tm, tn), functools.partial(lambda i, j, b: (i, b * nj + j), b=b)) for b in range(nb)]
    return pl.pallas_call(
        functools.partial(_merge_kernel, nb=nb),
        grid=(M // tm, nj),
        in_specs=y_specs + g_specs + [pl.BlockSpec((nb, W, tn), lambda i, j: (0, 0, j))],
        out_specs=pl.BlockSpec((tm, tn), lambda i, j: (i, j)),
        out_shape=jax.ShapeDtypeStruct((M, D), bf16),
        compiler_params=pltpu.CompilerParams(dimension_semantics=("parallel", "parallel"),
                                             vmem_limit_bytes=VMEM_LIMIT),
        name="merge")(*ys, *([gates] * nb), w_branch.astype(bf16))


def _rwkv_kernel(lw_ref, k_ref, v_ref, a_ref, b_ref, r_ref, y_ref, h_ref, *, n_heads):
    T = RW_CHUNK
    dh = HEAD_DIM

    @pl.when(pl.program_id(1) == 0)
    def _():
        h_ref[...] = jnp.zeros_like(h_ref)

    row = lax.broadcasted_iota(jnp.int32, (T, T), 0)
    col = lax.broadcasted_iota(jnp.int32, (T, T), 1)
    incl = row >= col
    strict = row > col
    eye = row == col
    tri = jnp.where(incl, 1.0, 0.0).astype(bf16)

    lw = lw_ref[0]
    c = _split_dot(tri, lw)
    c_end = c[T - 1:T, :]
    e_inc = jnp.exp(c)
    e_exc = jnp.exp(c - lw)
    e_neg = jnp.exp(-c)
    e_rem = jnp.exp(c_end - c)
    g_end = jnp.exp(c_end)
    kk = k_ref[0]
    bb = b_ref[0]
    at_all = a_ref[0] * e_exc
    rt_all = r_ref[0] * e_inc
    bt_all = bb * e_neg
    kt_all = kk * e_neg
    bp_all = bb * e_rem
    kp_all = kk * e_rem
    v_all = v_ref[0]

    for h in range(n_heads):
        sl = slice(h * dh, (h + 1) * dh)
        at, rt, bt, kt = at_all[:, sl], rt_all[:, sl], bt_all[:, sl], kt_all[:, sl]
        bp, kp, v = bp_all[:, sl], kp_all[:, sl], v_all[:, sl]
        g = _bdot_nt(jnp.concatenate([at, rt], axis=0), jnp.concatenate([bt, kt], axis=0))
        a_ab = jnp.where(strict, g[:T, :T], 0.0)
        a_ak = jnp.where(strict, g[:T, T:], 0.0)
        m_rb = jnp.where(incl, g[T:, :T], 0.0)
        m_rk = jnp.where(incl, g[T:, T:], 0.0)
        x = jnp.concatenate([at, _bdot(a_ak, v)], axis=1)
        p = a_ab
        n = 1
        while True:
            x = x + _bdot(p, x)
            n *= 2
            if n >= T:
                break
            p = _bdot(p, p)
        mx = _bdot(m_rb, x)
        r_hat = rt + mx[:, :dh]
        y0 = mx[:, dh:] + _bdot(m_rk, v)
        bx = _bdot_tn(bp, x)
        p_mat = jnp.where(eye, g_end[:, sl], 0.0) + bx[:, :dh]
        q_mat = bx[:, dh:] + _bdot_tn(kp, v)
        h0 = h_ref[h]
        yh = _bdot(jnp.concatenate([r_hat, p_mat], axis=0), h0)
        y_ref[0, :, sl] = yh[:T] + y0
        h_ref[h] = yh[T:] + q_mat


def _rwkv_scan(lw, k, v, a, b, r):
    G, L, W = k.shape
    n_heads = W // HEAD_DIM
    spec = pl.BlockSpec((1, RW_CHUNK, W), lambda g, c: (g, c, 0))
    return pl.pallas_call(
        functools.partial(_rwkv_kernel, n_heads=n_heads),
        grid=(G, L // RW_CHUNK),
        in_specs=[spec] * 6,
        out_specs=spec,
        out_shape=jax.ShapeDtypeStruct((G, L, W), f32),
        scratch_shapes=[pltpu.VMEM((n_heads, HEAD_DIM, HEAD_DIM), f32)],
        compiler_params=pltpu.CompilerParams(dimension_semantics=("parallel", "arbitrary"),
                                             vmem_limit_bytes=VMEM_LIMIT),
        name="rwkv")(lw, k, v, a, b, r)


def _mlstm_kernel(q_ref, k_ref, v_ref, gc_ref, gr_ref, h_ref, c_ref, n_ref, m_ref, *, n_heads):
    T = ML_CHUNK
    dh = HEAD_DIM

    @pl.when(pl.program_id(1) == 0)
    def _():
        c_ref[...] = jnp.zeros_like(c_ref)
        n_ref[...] = jnp.zeros_like(n_ref)
        m_ref[...] = jnp.zeros_like(m_ref)

    row = lax.broadcasted_iota(jnp.int32, (T, T), 0)
    col = lax.broadcasted_iota(jnp.int32, (T, T), 1)
    lower = row >= col
    tri_l = jnp.where(lower, 1.0, 0.0).astype(bf16)
    tri_u = jnp.where(row <= col, 1.0, 0.0).astype(bf16)

    gc = gc_ref[0]
    gr = gr_ref[0, 0]
    li_c = gc[:, :n_heads]
    b_c = _split_dot(tri_l, gc[:, n_heads:])
    li_r = gr[:n_heads, :]
    b_r = _split_dot_r(gr[n_heads:, :], tri_u)

    q_all = q_ref[0]
    k_all = k_ref[0]
    v_all = v_ref[0]
    for h in range(n_heads):
        sl = slice(h * dh, (h + 1) * dh)
        q, k, v = q_all[:, sl], k_all[:, sl], v_all[:, sl]
        bc = b_c[:, h:h + 1]
        lic = li_c[:, h:h + 1]
        br = b_r[h:h + 1, :]
        lir = li_r[h:h + 1, :]
        m = m_ref[h:h + 1, 0:1]
        ct = c_ref[h]
        nrow = n_ref[h:h + 1, :]
        b_end = bc[T - 1:T, :]
        gcol = bc + m
        log_d = jnp.where(lower, bc - br + lir, NEG_BIG)
        m_t = jnp.maximum(gcol, jnp.max(log_d, axis=1, keepdims=True))
        inter = jnp.exp(gcol - m_t)
        qk = _bdot_nt(q, k) * jnp.exp(log_d - m_t)
        num = inter * _bdot(q, ct) + _bdot(qk, v)
        den = inter * jnp.sum(q * nrow, axis=1, keepdims=True) + jnp.sum(qk, axis=1, keepdims=True)
        h_ref[0, :, sl] = num / jnp.maximum(jnp.abs(den), jnp.exp(-m_t))
        log_in = b_end - bc + lic
        m_new = jnp.maximum(b_end + m, jnp.max(log_in, axis=0, keepdims=True))
        carry = jnp.exp(b_end + m - m_new)
        kw = k * jnp.exp(log_in - m_new)
        c_ref[h] = carry * ct + _bdot_tn(kw, v)
        n_ref[h:h + 1, :] = carry * nrow + jnp.sum(kw, axis=0, keepdims=True)
        m_ref[h:h + 1, :] = jnp.broadcast_to(m_new, (1, LANE))


def _mlstm_scan(q, k, v, li, lf):
    G, L, W = k.shape
    H = W // HEAD_DIM
    nc = L // ML_CHUNK
    gcol = jnp.concatenate([li, lf], axis=-1)
    grow = jnp.swapaxes(gcol.reshape(G, nc, ML_CHUNK, 2 * H), 2, 3)
    spec = pl.BlockSpec((1, ML_CHUNK, W), lambda g, c: (g, c, 0))
    return pl.pallas_call(
        functools.partial(_mlstm_kernel, n_heads=H),
        grid=(G, nc),
        in_specs=[spec, spec, spec,
                  pl.BlockSpec((1, ML_CHUNK, 2 * H), lambda g, c: (g, c, 0)),
                  pl.BlockSpec((1, 1, 2 * H, ML_CHUNK), lambda g, c: (g, c, 0, 0))],
        out_specs=spec,
        out_shape=jax.ShapeDtypeStruct((G, L, W), f32),
        scratch_shapes=[pltpu.VMEM((H, HEAD_DIM, HEAD_DIM), f32),
                        pltpu.VMEM((H, HEAD_DIM), f32),
                        pltpu.VMEM((H, LANE), f32)],
        compiler_params=pltpu.CompilerParams(dimension_semantics=("parallel", "arbitrary"),
                                             vmem_limit_bytes=VMEM_LIMIT),
        name="mlstm")(q, k, v, gcol, grow)


def _na_row_start(r, n_rows, win_r):
    return jnp.clip(r - win_r // 2, 0, n_rows - win_r)


def _na_kernel(qr_ref, qp_ref, k_ref, v_ref, kc_ref, vc_ref, bias_ref, o_ref, *, n_heads, n_rows, win_r):
    dh = HEAD_DIM
    r = pl.program_id(1)
    start = pl.multiple_of(_na_row_start(r, n_rows, win_r) * GRID_W, GRID_W)
    kwin = k_ref[0, pl.ds(start, win_r * GRID_W), :]
    vwin = v_ref[0, pl.ds(start, win_r * GRID_W), :]
    qr = qr_ref[0]
    qp = qp_ref[0]
    kc = kc_ref[0]
    vc = vc_ref[0]
    for h in range(n_heads):
        sl = slice(h * dh, (h + 1) * dh)
        s_loc = _bdot_nt(qr[:, sl], kwin[:, sl]) + bias_ref[h, 0]
        s_ctx = _bdot_nt(qp[:, sl], kc[:, sl])
        m = jnp.maximum(jnp.max(s_loc, axis=1, keepdims=True), jnp.max(s_ctx, axis=1, keepdims=True))
        p_loc = jnp.exp(s_loc - m)
        p_ctx = jnp.exp(s_ctx - m)
        den = jnp.sum(p_loc, axis=1, keepdims=True) + jnp.sum(p_ctx, axis=1, keepdims=True)
        o = _bdot(p_loc, vwin[:, sl]) + _bdot(p_ctx, vc[:, sl])
        o_ref[0, :, sl] = o / den


def _na_attention(q_rot, q_plain, k_rot, v, kc, vc, bias_tab, n_rows, win_r):
    B, S, W = q_rot.shape
    Lc = kc.shape[1]
    H = W // HEAD_DIM
    n_var = bias_tab.shape[1]
    qspec = pl.BlockSpec((1, GRID_W, W), lambda b, r: (b, r, 0))
    full = pl.BlockSpec((1, S, W), lambda b, r: (b, 0, 0))
    cspec = pl.BlockSpec((1, Lc, W), lambda b, r: (b, 0, 0))

    def bias_map(b, r):
        return (0, _na_row_start(r, n_rows, win_r) - r + win_r - 1, 0, 0)

    return pl.pallas_call(
        functools.partial(_na_kernel, n_heads=H, n_rows=n_rows, win_r=win_r),
        grid=(B, n_rows),
        in_specs=[qspec, qspec, full, full, cspec, cspec,
                  pl.BlockSpec((H, 1, GRID_W, win_r * GRID_W), bias_map)],
        out_specs=qspec,
        out_shape=jax.ShapeDtypeStruct((B, S, W), f32),
        compiler_params=pltpu.CompilerParams(dimension_semantics=("parallel", "arbitrary"),
                                             vmem_limit_bytes=VMEM_LIMIT),
        name="na")(q_rot, q_plain, k_rot, v, kc, vc, bias_tab)


def _ctx_attn_kernel(q_ref, k_ref, v_ref, o_ref, *, n_heads):
    dh = HEAD_DIM
    q = q_ref[0]
    k = k_ref[0]
    v = v_ref[0]
    for h in range(n_heads):
        sl = slice(h * dh, (h + 1) * dh)
        s = _bdot_nt(q[:, sl], k[:, sl])
        m = jnp.max(s, axis=1, keepdims=True)
        p = jnp.exp(s - m)
        o_ref[0, :, sl] = _bdot(p, v[:, sl]) / jnp.sum(p, axis=1, keepdims=True)


def _ctx_attention(q, k, v):
    B, Lc, W = q.shape
    spec = pl.BlockSpec((1, Lc, W), lambda b: (b, 0, 0))
    return pl.pallas_call(
        functools.partial(_ctx_attn_kernel, n_heads=W // HEAD_DIM),
        grid=(B,), in_specs=[spec] * 3, out_specs=spec,
        out_shape=jax.ShapeDtypeStruct((B, Lc, W), f32),
        compiler_params=pltpu.CompilerParams(dimension_semantics=("parallel",), vmem_limit_bytes=VMEM_LIMIT),
        name="ctx_attn")(q, k, v)


def _moe_kernel(be_ref, nused_ref, x_ref, g_ref, wgu_ref, bgu_ref, wdn_ref, bdn_ref, o_ref, *, d_expert):
    i = pl.program_id(0)

    @pl.when(i < nused_ref[0])
    def _():
        gu = jnp.dot(x_ref[...], wgu_ref[0], preferred_element_type=f32) + bgu_ref[0]
        gt = jnp.minimum(gu[:, :d_expert], SWIGLU_LIMIT)
        up = jnp.clip(gu[:, d_expert:], -SWIGLU_LIMIT, SWIGLU_LIMIT)
        act = (up + 1.0) * (gt / (1.0 + jnp.exp(-SWIGLU_ALPHA * gt)))
        y = jnp.dot(act.astype(bf16), wdn_ref[0], preferred_element_type=f32) + bdn_ref[0]
        o_ref[...] = y * g_ref[...]

    @pl.when(i >= nused_ref[0])
    def _():
        o_ref[...] = jnp.zeros_like(o_ref)


def _moe_experts(xg, g_buf, block_e, n_used, w_gu, b_gu, w_dn, b_dn):
    R, D = xg.shape
    n_blocks = R // MOE_BLOCK
    E, _, F2 = w_gu.shape
    grid_spec = pltpu.PrefetchScalarGridSpec(
        num_scalar_prefetch=2,
        grid=(n_blocks,),
        in_specs=[pl.BlockSpec((MOE_BLOCK, D), lambda i, be, nu: (i, 0)),
                  pl.BlockSpec((MOE_BLOCK, 1), lambda i, be, nu: (i, 0)),
                  pl.BlockSpec((1, D, F2), lambda i, be, nu: (be[i], 0, 0)),
                  pl.BlockSpec((1, 1, F2), lambda i, be, nu: (be[i], 0, 0)),
                  pl.BlockSpec((1, F2 // 2, D), lambda i, be, nu: (be[i], 0, 0)),
                  pl.BlockSpec((1, 1, D), lambda i, be, nu: (be[i], 0, 0))],
        out_specs=pl.BlockSpec((MOE_BLOCK, D), lambda i, be, nu: (i, 0)))
    return pl.pallas_call(
        functools.partial(_moe_kernel, d_expert=F2 // 2),
        grid_spec=grid_spec,
        out_shape=jax.ShapeDtypeStruct((R, D), f32),
        compiler_params=pltpu.CompilerParams(dimension_semantics=("arbitrary",), vmem_limit_bytes=VMEM_LIMIT),
        name="moe")(block_e, n_used, xg, g_buf[:, None], w_gu.astype(bf16), b_gu[:, None, :],
                    w_dn.astype(bf16), b_dn[:, None, :])


def _moe(h, w_router, b_router, w_gu, b_gu, w_dn, b_dn):
    T, D = h.shape
    E = w_router.shape[1]
    h_hi = h.astype(bf16)
    h_lo = (h - h_hi.astype(f32)).astype(bf16)
    w_hi = w_router.astype(bf16)
    w_lo = (w_router - w_hi.astype(f32)).astype(bf16)
    logits = _mm(h_hi, w_hi) + _mm(h_lo, w_hi) + _mm(h_hi, w_lo) + b_router
    top_val, top_idx = lax.top_k(logits, TOP_K)
    gate = jax.nn.softmax(top_val, axis=-1)
    flat_e = top_idx.reshape(-1)
    flat_g = gate.reshape(-1)
    order = jnp.argsort(flat_e)
    e_sorted = flat_e[order]
    counts = jnp.bincount(flat_e, length=E)
    padded = (counts + MOE_BLOCK - 1) // MOE_BLOCK * MOE_BLOCK
    pad_end = jnp.cumsum(padded)
    pad_start = pad_end - padded
    start = jnp.cumsum(counts) - counts
    dest = (pad_start[e_sorted] + jnp.arange(T * TOP_K, dtype=jnp.int32) - start[e_sorted]).astype(jnp.int32)
    n_blocks = (T * TOP_K + MOE_BLOCK - 1) // MOE_BLOCK + E
    tok_sorted = (order // TOP_K).astype(jnp.int32)
    tok_buf = jnp.zeros((n_blocks * MOE_BLOCK,), jnp.int32).at[dest].set(tok_sorted)
    g_buf = jnp.zeros((n_blocks * MOE_BLOCK,), f32).at[dest].set(flat_g[order])
    block_e = jnp.minimum(jnp.searchsorted(pad_end, jnp.arange(n_blocks) * MOE_BLOCK, side='right'),
                          E - 1).astype(jnp.int32)
    n_used = (pad_end[-1:] // MOE_BLOCK).astype(jnp.int32)
    xg = h_hi[tok_buf]
    ys = _moe_experts(xg, g_buf, block_e, n_used, w_gu, b_gu, w_dn, b_dn)
    pos = jnp.zeros((T * TOP_K,), jnp.int32).at[order].set(dest).reshape(T, TOP_K)
    out = ys[pos[:, 0]]
    for j in range(1, TOP_K):
        out = out + ys[pos[:, j]]
    return out


def _rms_norm(x, g):
    return x * lax.rsqrt(jnp.mean(x * x, -1, keepdims=True) + NORM_EPS) * g


def _group_norm(y, g, b, n_heads, eps):
    shp = y.shape
    yh = y.reshape(shp[:-1] + (n_heads, shp[-1] // n_heads))
    mu = jnp.mean(yh, -1, keepdims=True)
    var = jnp.mean(jnp.square(yh - mu), -1, keepdims=True)
    out = ((yh - mu) * lax.rsqrt(var + eps)).reshape(shp) * g
    return out if b is None else out + b


def _dft_mats(n, scale):
    j = jnp.arange(n, dtype=jnp.int32)
    m = (j[:, None] * j[None, :]) % n
    ang = m.astype(f32) * (2.0 * math.pi / n)
    return jnp.cos(ang) * scale, jnp.sin(ang) * scale


def _axial_rope(t, rows, cols):
    half = HEAD_DIM // 2
    nf = half // 2
    inv = ROPE_THETA ** (-jnp.arange(nf, dtype=f32) / nf)

    def rot(u, pos):
        ang = pos.astype(f32)[:, None] * inv
        cos = jnp.cos(ang)[None, :, None, :]
        sin = jnp.sin(ang)[None, :, None, :]
        u1, u2 = u[..., :nf], u[..., nf:]
        return jnp.concatenate([u1 * cos - u2 * sin, u1 * sin + u2 * cos], -1)

    return jnp.concatenate([rot(t[..., :half], rows), rot(t[..., half:], cols)], -1)


def _seg_map(fn, u, Lc):
    return jnp.concatenate([fn(u[:, :Lc]), fn(u[:, Lc:])], axis=1)


def _shift(u, direction):
    if direction == 0:
        return jnp.pad(u[:, :-1], ((0, 0), (1, 0), (0, 0)))
    return jnp.pad(u[:, 1:], ((0, 0), (0, 1), (0, 0)))


def _conv3(u, w):
    return w[0] * _shift(u, 0) + w[1] * u + w[2] * _shift(u, 1)


def _na_bias_table(rpb, win_r):
    H = rpb.shape[0]
    w = jnp.arange(GRID_W)
    col_start = jnp.clip(w - NA_WIN_COLS // 2, 0, GRID_W - NA_WIN_COLS)
    col_in = (w[None, :] >= col_start[:, None]) & (w[None, :] < col_start[:, None] + NA_WIN_COLS)
    col_off = jnp.clip(w[None, :] - w[:, None], 1 - NA_WIN_COLS, NA_WIN_COLS - 1) + NA_WIN_COLS - 1
    per_row = jnp.where(col_in[None, None], rpb[:, :, col_off], NEG_BIG)
    tabs = []
    for j in range(win_r):
        d0 = j - (win_r - 1)
        rows = [per_row[:, d0 + a + NA_WIN_ROWS - 1] for a in range(win_r)]
        tabs.append(jnp.concatenate(rows, axis=-1))
    return jnp.stack(tabs, axis=1)


def _mixer(hcat, lp, B, Lc, S, last):
    D = hcat.shape[-1]
    BW = D // N_BRANCHES
    H = BW // HEAD_DIM
    Lt = Lc + S
    M = B * Lt
    state_w = 6 * BW + 2 * RW_DECAY_LORA + 2 * RW_ICL_LORA + 4 * H
    pad = (-state_w) % LANE
    w_in = lp['w_in']
    w2 = jnp.concatenate([w_in[:, :state_w], jnp.zeros((D, pad), w_in.dtype), w_in[:, state_w:]], axis=1)
    px = _mm(hcat.reshape(M, D), w2).reshape(B, Lt, -1)

    sizes = (BW, BW, BW, BW, 2 * RW_DECAY_LORA, 2 * RW_ICL_LORA, BW, BW, 4 * H + pad,
             BW, BW, BW, RW_GATE_LORA, BW, BW, N_BRANCHES * D)
    parts, o = [], 0
    for s in sizes:
        parts.append(px[..., o:o + s])
        o += s
    (na_k, na_v, rw_k, rw_v, rw_wl, rw_al, ml_k, ml_v, ml_g,
     fo, na_q, rw_r, rw_gl, ml_q, ml_o, gates) = parts
    ml_g = ml_g[..., :4 * H]

    C = BW // FOURIER_GROUPS
    cc, sc = _dft_mats(C, C ** -0.5)
    eye_g = jnp.eye(FOURIER_GROUPS, dtype=f32)
    chan = jnp.concatenate([jnp.kron(eye_g, cc), -jnp.kron(eye_g, sc)], axis=1)
    uc = _mm(fo.reshape(M, BW), chan).reshape(B, Lt, 2 * BW)

    def pos_dft(u, L):
        cl, sl_ = _dft_mats(L, L ** -0.5)
        rhs = jnp.concatenate([jnp.swapaxes(u[..., :BW], 0, 1).reshape(L, B * BW),
                               jnp.swapaxes(u[..., BW:], 0, 1).reshape(L, B * BW)], axis=0)
        y = _mm(jnp.concatenate([cl, sl_], axis=1), rhs)
        return jnp.swapaxes(y.reshape(L, B, BW), 0, 1)

    y_four = jnp.concatenate([pos_dft(uc[:, :Lc], Lc), pos_dft(uc[:, Lc:], S)], axis=1)

    n_rows = S // GRID_W
    win_r = min(NA_WIN_ROWS, n_rows)
    scale = HEAD_DIM ** -0.5
    t = jnp.arange(S)
    q_lat = na_q[:, Lc:].reshape(B, S, H, HEAD_DIM)
    k_lat = na_k[:, Lc:].reshape(B, S, H, HEAD_DIM)
    q_rot = (_axial_rope(q_lat, t // GRID_W, t % GRID_W) * scale).reshape(B, S, BW).astype(bf16)
    k_rot = _axial_rope(k_lat, t // GRID_W, t % GRID_W).reshape(B, S, BW).astype(bf16)
    q_plain = (na_q[:, Lc:] * scale).astype(bf16)
    kc = na_k[:, :Lc].astype(bf16)
    vc = na_v[:, :Lc].astype(bf16)
    bias_tab = _na_bias_table(lp['na_rpb'], win_r)
    y_na_x = _na_attention(q_rot, q_plain, k_rot, na_v[:, Lc:].astype(bf16), kc, vc, bias_tab, n_rows, win_r)
    if last:
        y_na_c = jnp.zeros((B, Lc, BW), f32)
    else:
        y_na_c = _ctx_attention((na_q[:, :Lc] * scale).astype(bf16), kc, vc)
    y_na = jnp.concatenate([y_na_c, y_na_x], axis=1)

    def flip_seg(u):
        return _seg_map(lambda s: jnp.flip(s, axis=1), u, Lc)

    seqs = []
    bonus_in = []
    for d in (0, 1):
        mu = lp['rw_mu_rkv'][d]
        mu_wa = lp['rw_mu_wa'][d]

        def lerp(u, m_):
            return u + m_ * (_seg_map(lambda s: _shift(s, d), u, Lc) - u)

        r_l = lerp(rw_r, mu[:BW])
        k_l = lerp(rw_k, mu[BW:2 * BW])
        v_l = lerp(rw_v, mu[2 * BW:])
        wl_l = lerp(rw_wl[..., d * RW_DECAY_LORA:(d + 1) * RW_DECAY_LORA], mu_wa[:RW_DECAY_LORA])
        al_l = lerp(rw_al[..., d * RW_ICL_LORA:(d + 1) * RW_ICL_LORA], mu_wa[RW_DECAY_LORA:])
        w_pre = lp['rw_w0'][d] + _mm(jnp.tanh(wl_l).reshape(M, -1), lp['rw_w_up'][d]).reshape(B, Lt, BW)
        w_log = -jax.nn.softplus(-w_pre) - 0.5
        lw = -jnp.exp(w_log)
        a_icl = jax.nn.sigmoid(lp['rw_a0'][d] + _mm(al_l.reshape(M, -1), lp['rw_a_up'][d]).reshape(B, Lt, BW))
        kk = (k_l * lp['rw_k_k']).reshape(B, Lt, H, HEAD_DIM)
        kk = (kk / jnp.maximum(jnp.sqrt(jnp.sum(kk * kk, -1, keepdims=True)), 1e-12)).reshape(B, Lt, BW)
        k_mod = k_l * (1.0 + (a_icl - 1.0) * lp['rw_k_a'])
        arrs = (lw, k_mod, v_l, -kk, kk * a_icl, r_l)
        if d == 1:
            arrs = tuple(flip_seg(u) for u in arrs)
        seqs.append(arrs)
        bonus_in.append((r_l, k_mod, v_l))
    stacked = [jnp.concatenate([seqs[0][i], seqs[1][i]], axis=0) for i in range(6)]
    ys = _rwkv_scan(*stacked)
    y_rw = 0.0
    for d in (0, 1):
        y_d = ys[d * B:(d + 1) * B]
        if d == 1:
            y_d = flip_seg(y_d)
        y_d = _group_norm(y_d, lp['rw_gn_g'][d], lp['rw_gn_b'][d], H, RW_GN_EPS)
        r_l, k_mod, v_l = bonus_in[d]
        hs = (B, Lt, H, HEAD_DIM)
        bonus = jnp.sum(r_l.reshape(hs) * k_mod.reshape(hs) * lp['rw_r_k'][d], -1, keepdims=True) * v_l.reshape(hs)
        y_rw = y_rw + y_d + bonus.reshape(B, Lt, BW)
    y_rw = y_rw * _mm(jax.nn.sigmoid(rw_gl).reshape(M, -1), lp['rw_g_up']).reshape(B, Lt, BW)

    k_m = jax.nn.silu(_seg_map(lambda s: _conv3(s, lp['ml_conv_k']), ml_k, Lc)) * scale
    q_m = jax.nn.silu(_seg_map(lambda s: _conv3(s, lp['ml_conv_q']), ml_q, Lc))
    g4 = ml_g.reshape(B, Lt, 4, H) + lp['ml_gate_b']
    li = g4[:, :, :2]
    lf = jax.nn.log_sigmoid(g4[:, :, 2:])
    q_s = jnp.concatenate([q_m, flip_seg(q_m)], axis=0)
    k_s = jnp.concatenate([k_m, flip_seg(k_m)], axis=0)
    v_s = jnp.concatenate([ml_v, flip_seg(ml_v)], axis=0)
    li_s = jnp.concatenate([li[:, :, 0], flip_seg(li[:, :, 1])], axis=0)
    lf_s = jnp.concatenate([lf[:, :, 0], flip_seg(lf[:, :, 1])], axis=0)
    hs_ = _mlstm_scan(q_s, k_s, v_s, li_s, lf_s)
    h_sum = hs_[:B] + flip_seg(hs_[B:])
    y_ml = _group_norm(h_sum, lp['ml_gn_g'], None, H, GN_EPS) * jax.nn.sigmoid(ml_o)

    ys4 = [u.reshape(M, BW).astype(bf16) for u in (y_four, y_na, y_rw, y_ml)]
    acc = _merge(ys4, gates.reshape(M, N_BRANCHES * D), lp['w_branch'])
    return _mm(acc, lp['w_out']).reshape(B, Lt, D)


def kernel(x, c, ctx, c_ctx, w_ada, b_ada, g_norm1, g_norm2, w_in, na_rpb, rw_mu_rkv, rw_mu_wa, rw_w_up, rw_w0, rw_a_up, rw_a0, rw_g_up, rw_k_k, rw_k_a, rw_r_k, rw_gn_g, rw_gn_b, ml_conv_q, ml_conv_k, ml_gate_b, ml_gn_g, w_branch, w_out, w_router, b_router, w_gu, b_gu, w_dn, b_dn, g_final):
    B, S, D = x.shape
    Lc = ctx.shape[1]
    depth = w_ada.shape[0]
    mod_in = jnp.concatenate([jax.nn.silu(c), jax.nn.silu(c_ctx)[None]], axis=0)
    mod_in = jnp.pad(mod_in, ((0, (-(B + 1)) % 16), (0, 0)))
    for l in range(depth):
        last = l == depth - 1
        lp = {'w_in': w_in[l], 'na_rpb': na_rpb[l], 'rw_mu_rkv': rw_mu_rkv[l], 'rw_mu_wa': rw_mu_wa[l],
              'rw_w_up': rw_w_up[l], 'rw_w0': rw_w0[l], 'rw_a_up': rw_a_up[l], 'rw_a0': rw_a0[l],
              'rw_g_up': rw_g_up[l], 'rw_k_k': rw_k_k[l], 'rw_k_a': rw_k_a[l], 'rw_r_k': rw_r_k[l],
              'rw_gn_g': rw_gn_g[l], 'rw_gn_b': rw_gn_b[l], 'ml_conv_q': ml_conv_q[l],
              'ml_conv_k': ml_conv_k[l], 'ml_gate_b': ml_gate_b[l], 'ml_gn_g': ml_gn_g[l],
              'w_branch': w_branch[l], 'w_out': w_out[l]}
        mod = _mm(mod_in, w_ada[l]) + b_ada[l]
        mx = jnp.split(mod[:B], 6, axis=-1)
        mc = jnp.split(mod[B], 6, axis=-1)
        hx = _rms_norm(x, g_norm1[l]) * (1.0 + mx[1][:, None]) + mx[0][:, None]
        hc = _rms_norm(ctx, g_norm1[l]) * (1.0 + mc[1]) + mc[0]
        y = _mixer(jnp.concatenate([hc, hx], axis=1), lp, B, Lc, S, last)
        x = x + mx[2][:, None] * y[:, Lc:]
        hx2 = _rms_norm(x, g_norm2[l]) * (1.0 + mx[4][:, None]) + mx[3][:, None]
        if last:
            ym = _moe(hx2.reshape(-1, D), w_router[l], b_router[l], w_gu[l], b_gu[l], w_dn[l], b_dn[l])
            x = x + mx[5][:, None] * ym.reshape(x.shape)
        else:
            ctx = ctx + mc[2] * y[:, :Lc]
            hc2 = _rms_norm(ctx, g_norm2[l]) * (1.0 + mc[4]) + mc[3]
            n_x = B * S
            ym = _moe(jnp.concatenate([hx2.reshape(-1, D), hc2.reshape(-1, D)], axis=0),
                      w_router[l], b_router[l], w_gu[l], b_gu[l], w_dn[l], b_dn[l])
            x = x + mx[5][:, None] * ym[:n_x].reshape(x.shape)
            ctx = ctx + mc[5] * ym[n_x:].reshape(ctx.shape)
    return _rms_norm(x, g_final)
```

```python
import functools
import math

import jax
import jax.numpy as jnp
from jax import lax
from jax.experimental import pallas as pl
from jax.experimental.pallas import tpu as pltpu

f32 = jnp.float32
bf16 = jnp.bfloat16

HEAD_DIM = 64
GRID_W = 64
N_BRANCHES = 4
FOURIER_GROUPS = 4
NA_WIN_ROWS = 8
NA_WIN_COLS = 16
ROPE_THETA = 10000.0
RW_DECAY_LORA = 64
RW_ICL_LORA = 64
RW_GATE_LORA = 128
RW_GN_EPS = 64e-5
ML_CHUNK = 64
RW_CHUNK = 64
TOP_K = 4
SWIGLU_LIMIT = 7.0
SWIGLU_ALPHA = 1.702
MOE_BLOCK = 256
NORM_EPS = 1e-6
GN_EPS = 1e-5
NEG_BIG = -1e30

LANE = 128
VMEM_LIMIT = 56 * 1024 * 1024


def _bdot(a, b):
    return jnp.dot(a.astype(bf16), b.astype(bf16), preferred_element_type=f32)


def _bdot_nt(a, b):
    return lax.dot_general(a.astype(bf16), b.astype(bf16), (((1,), (1,)), ((), ())),
                           preferred_element_type=f32)


def _bdot_tn(a, b):
    return lax.dot_general(a.astype(bf16), b.astype(bf16), (((0,), (0,)), ((), ())),
                           preferred_element_type=f32)


def _split_dot(tri, x):
    hi = x.astype(bf16)
    lo = (x - hi.astype(f32)).astype(bf16)
    return (jnp.dot(tri, hi, preferred_element_type=f32) + jnp.dot(tri, lo, preferred_element_type=f32))


def _split_dot_r(x, tri):
    hi = x.astype(bf16)
    lo = (x - hi.astype(f32)).astype(bf16)
    return (jnp.dot(hi, tri, preferred_element_type=f32) + jnp.dot(lo, tri, preferred_element_type=f32))


def _pick(dim, cap, align):
    if dim <= cap:
        return dim
    t = cap - cap % align
    while t >= align:
        if dim % t == 0:
            return t
        t -= align
    return dim


def _mm_kernel_single(a_ref, b_ref, o_ref):
    o_ref[...] = jnp.dot(a_ref[...], b_ref[...], preferred_element_type=f32).astype(o_ref.dtype)


def _mm_kernel_acc(a_ref, b_ref, o_ref, acc_ref, *, nk):
    k = pl.program_id(2)

    @pl.when(k == 0)
    def _():
        acc_ref[...] = jnp.zeros_like(acc_ref)

    acc_ref[...] += jnp.dot(a_ref[...], b_ref[...], preferred_element_type=f32)

    @pl.when(k == nk - 1)
    def _():
        o_ref[...] = acc_ref[...].astype(o_ref.dtype)


def _mm(a, b, out_dtype=f32):
    a = a.astype(bf16)
    b = b.astype(bf16)
    M, K = a.shape
    N = b.shape[1]
    tm = _pick(M, 1024, 16)
    tn = _pick(N, 1024, LANE)
    tk = _pick(K, 2048, LANE)
    nk = K // tk
    grid = (M // tm, N // tn, nk)
    in_specs = [pl.BlockSpec((tm, tk), lambda i, j, k: (i, k)),
                pl.BlockSpec((tk, tn), lambda i, j, k: (k, j))]
    out_spec = pl.BlockSpec((tm, tn), lambda i, j, k: (i, j))
    params = pltpu.CompilerParams(dimension_semantics=("parallel", "parallel", "arbitrary"),
                                  vmem_limit_bytes=VMEM_LIMIT)
    if nk == 1:
        return pl.pallas_call(_mm_kernel_single, grid=grid, in_specs=in_specs, out_specs=out_spec,
                              out_shape=jax.ShapeDtypeStruct((M, N), out_dtype),
                              compiler_params=params, name="mm")(a, b)
    return pl.pallas_call(functools.partial(_mm_kernel_acc, nk=nk), grid=grid, in_specs=in_specs,
                          out_specs=out_spec, out_shape=jax.ShapeDtypeStruct((M, N), out_dtype),
                          scratch_shapes=[pltpu.VMEM((tm, tn), f32)],
                          compiler_params=params, name="mm_acc")(a, b)


def _mm_ws_kernel(a_ref, b_ref, o_ref, w_ref):
    @pl.when(pl.program_id(1) == 0)
    def _():
        w_ref[...] = b_ref[...].astype(bf16)

    o_ref[...] = jnp.dot(a_ref[...], w_ref[...], preferred_element_type=f32).astype(o_ref.dtype)


def _mm_w(a, w, col_off=0, n_cols=None, out_dtype=f32):
    a = a.astype(bf16)
    M, K = a.shape
    n_cols = w.shape[1] - col_off if n_cols is None else n_cols
    tm = _pick(M, 1024, 16)
    tn = _pick(n_cols, 512, LANE)
    if col_off % tn:
        tn = math.gcd(tn, col_off)
    off = col_off // tn
    return pl.pallas_call(
        _mm_ws_kernel,
        grid=(n_cols // tn, M // tm),
        in_specs=[pl.BlockSpec((tm, K), lambda j, i: (i, 0)),
                  pl.BlockSpec((K, tn), lambda j, i: (0, j + off))],
        out_specs=pl.BlockSpec((tm, tn), lambda j, i: (i, j)),
        out_shape=jax.ShapeDtypeStruct((M, n_cols), out_dtype),
        scratch_shapes=[pltpu.VMEM((K, tn), bf16)],
        compiler_params=pltpu.CompilerParams(dimension_semantics=("parallel", "arbitrary"),
                                             vmem_limit_bytes=VMEM_LIMIT),
        name="mm_w")(a, w)


def _merge_kernel(*refs, nb):
    h_ref = refs[0]
    y_refs, g_refs = refs[1:1 + nb], refs[1 + nb:1 + 2 * nb]
    wb_ref, o_ref, wg_scr, wb_scr = refs[1 + 2 * nb:]

    @pl.when(pl.program_id(1) == 0)
    def _():
        for i in range(nb):
            wg_scr[i] = g_refs[i][...].astype(bf16)
        wb_scr[...] = wb_ref[...].astype(bf16)

    h = h_ref[...]
    acc = None
    for i in range(nb):
        gate = jnp.dot(h, wg_scr[i], preferred_element_type=f32)
        p = jnp.dot(y_refs[i][...], wb_scr[i], preferred_element_type=f32)
        t = p / (1.0 + jnp.exp(-gate))
        acc = t if acc is None else acc + t
    o_ref[...] = acc.astype(o_ref.dtype)


def _merge(h, ys, w_all, gate_off, w_branch):
    nb = len(ys)
    M, K = h.shape
    W = ys[0].shape[1]
    D = w_branch.shape[-1]
    tm = _pick(M, 1024, 16)
    tn = math.gcd(_pick(D, 256, LANE), gate_off)
    nj = D // tn
    off = gate_off // tn
    g_specs = [pl.BlockSpec((K, tn), functools.partial(lambda j, i, b: (0, off + b * nj + j), b=b))
               for b in range(nb)]
    return pl.pallas_call(
        functools.partial(_merge_kernel, nb=nb),
        grid=(nj, M // tm),
        in_specs=[pl.BlockSpec((tm, K), lambda j, i: (i, 0))]
                 + [pl.BlockSpec((tm, W), lambda j, i: (i, 0)) for _ in range(nb)]
                 + g_specs + [pl.BlockSpec((nb, W, tn), lambda j, i: (0, 0, j))],
        out_specs=pl.BlockSpec((tm, tn), lambda j, i: (i, j)),
        out_shape=jax.ShapeDtypeStruct((M, D), bf16),
        scratch_shapes=[pltpu.VMEM((nb, K, tn), bf16), pltpu.VMEM((nb, W, tn), bf16)],
        compiler_params=pltpu.CompilerParams(dimension_semantics=("parallel", "arbitrary"),
                                             vmem_limit_bytes=VMEM_LIMIT),
        name="merge")(h, *ys, *([w_all] * nb), w_branch)


def _rwkv_kernel(lw_ref, k_ref, v_ref, a_ref, b_ref, r_ref, y_ref, h_ref, *, n_heads, n_fwd):
    T = RW_CHUNK
    dh = HEAD_DIM
    hd = range(n_heads)

    @pl.when(pl.program_id(1) == 0)
    def _():
        h_ref[...] = jnp.zeros_like(h_ref)

    rev = pl.program_id(0) >= n_fwd
    row = lax.broadcasted_iota(jnp.int32, (T, T), 0)
    col = lax.broadcasted_iota(jnp.int32, (T, T), 1)
    sign = jnp.where(rev, -1, 1)
    strict = row * sign > col * sign
    eye = row == col
    incl = strict | eye
    tri = jnp.where(incl, 1.0, 0.0).astype(bf16)

    lw = lw_ref[0]
    c = _split_dot(tri, lw)
    c_end = jnp.where(rev, c[0:1, :], c[T - 1:T, :])
    e_inc = jnp.exp(c)
    e_exc = jnp.exp(c - lw)
    e_neg = jnp.exp(-c)
    e_rem = jnp.exp(c_end - c)
    g_end = jnp.exp(c_end)
    kk = k_ref[0]
    bb = b_ref[0]
    at_all = a_ref[0] * e_exc
    rt_all = r_ref[0] * e_inc
    bt_all = bb * e_neg
    kt_all = kk * e_neg
    bp_all = bb * e_rem
    kp_all = kk * e_rem
    v_all = v_ref[0]
    sls = [slice(h * dh, (h + 1) * dh) for h in hd]
    zeros = jnp.zeros((T, dh), f32)

    g = [_bdot_nt(jnp.concatenate([at_all[:, s], rt_all[:, s]], axis=0),
                  jnp.concatenate([bt_all[:, s], kt_all[:, s]], axis=0)) for s in sls]
    p = [jnp.where(strict, g[h][:T, :T], 0.0) for h in hd]
    u = [_bdot(jnp.where(strict, g[h][:T, T:], 0.0), v_all[:, sls[h]]) for h in hd]
    x = [jnp.concatenate([at_all[:, sls[h]], u[h]], axis=1) for h in hd]
    n = 1
    while 2 * n < T:
        px = [_bdot(p[h], jnp.concatenate([x[h], p[h]], axis=1)) for h in hd]
        x = [x[h] + px[h][:, :2 * dh] for h in hd]
        p = [px[h][:, 2 * dh:] for h in hd]
        n *= 2
    x = [x[h] + _bdot(p[h], x[h]) for h in hd]
    z = [jnp.concatenate([x[h], jnp.concatenate([zeros, v_all[:, sls[h]]], axis=1)], axis=0) for h in hd]
    mx = [_bdot(jnp.concatenate([jnp.where(incl, g[h][T:, :T], 0.0), jnp.where(incl, g[h][T:, T:], 0.0)], axis=1),
                z[h]) for h in hd]
    bx = [_bdot_tn(jnp.concatenate([bp_all[:, sls[h]], kp_all[:, sls[h]]], axis=0), z[h]) for h in hd]
    yh = [_bdot(jnp.concatenate([rt_all[:, sls[h]] + mx[h][:, :dh],
                                 jnp.where(eye, g_end[:, sls[h]], 0.0) + bx[h][:, :dh]], axis=0), h_ref[h])
          for h in hd]
    for h in hd:
        y_ref[0, :, sls[h]] = yh[h][:T] + mx[h][:, dh:]
        h_ref[h] = yh[h][T:] + bx[h][:, dh:]


def _seq_chunk_map(n_fwd, n_ctx, nc):
    def index_map(g, c):
        back = jnp.where(c < n_ctx, n_ctx - 1 - c, nc - 1 - c + n_ctx)
        return (g, jnp.where(g >= n_fwd, back, c), 0)
    return index_map


def _rwkv_scan(lw, k, v, a, b, r, n_ctx):
    G, L, W = k.shape
    n_heads = W // HEAD_DIM
    spec = pl.BlockSpec((1, RW_CHUNK, W), _seq_chunk_map(G // 2, n_ctx, L // RW_CHUNK))
    return pl.pallas_call(
        functools.partial(_rwkv_kernel, n_heads=n_heads, n_fwd=G // 2),
        grid=(G, L // RW_CHUNK),
        in_specs=[spec] * 6,
        out_specs=spec,
        out_shape=jax.ShapeDtypeStruct((G, L, W), f32),
        scratch_shapes=[pltpu.VMEM((n_heads, HEAD_DIM, HEAD_DIM), f32)],
        compiler_params=pltpu.CompilerParams(dimension_semantics=("parallel", "arbitrary"),
                                             vmem_limit_bytes=VMEM_LIMIT),
        name="rwkv")(lw, k, v, a, b, r)


def _mlstm_kernel(q_ref, k_ref, v_ref, gc_ref, gr_ref, h_ref, c_ref, m_ref, *, n_heads, n_fwd):
    T = ML_CHUNK
    dh = HEAD_DIM
    hd = range(n_heads)

    @pl.when(pl.program_id(1) == 0)
    def _():
        c_ref[...] = jnp.zeros_like(c_ref)
        m_ref[...] = jnp.zeros_like(m_ref)

    rev = pl.program_id(0) >= n_fwd
    sign = jnp.where(rev, -1, 1)
    row = lax.broadcasted_iota(jnp.int32, (T, T), 0)
    col = lax.broadcasted_iota(jnp.int32, (T, T), 1)
    lower = row * sign >= col * sign
    tri_l = jnp.where(lower, 1.0, 0.0).astype(bf16)
    tri_u = jnp.where(col * sign >= row * sign, 1.0, 0.0).astype(bf16)

    gc = gc_ref[0]
    gr = gr_ref[0, 0]
    li_c = gc[:, :n_heads]
    b_c = _split_dot(tri_l, gc[:, n_heads:])
    li_r = gr[:n_heads, :]
    b_r = _split_dot_r(gr[n_heads:, :], tri_u)
    b_end_all = jnp.where(rev, b_c[0:1, :], b_c[T - 1:T, :])

    q_all = q_ref[0]
    k_all = k_ref[0]
    v_all = v_ref[0]
    sls = [slice(h * dh, (h + 1) * dh) for h in hd]
    one_col = jnp.where(lax.broadcasted_iota(jnp.int32, (T, dh), 1) == 0, 1.0, 0.0)
    vaug = [jnp.concatenate([v_all[:, s], one_col], axis=1) for s in sls]
    bc = [b_c[:, h:h + 1] for h in hd]
    m = [m_ref[h:h + 1, 0:1] for h in hd]
    b_end = [b_end_all[:, h:h + 1] for h in hd]
    state = [c_ref[h] for h in hd]

    gcol = [bc[h] + m[h] for h in hd]
    log_d = [jnp.where(lower, bc[h] - b_r[h:h + 1, :] + li_r[h:h + 1, :], NEG_BIG) for h in hd]
    m_t = [jnp.maximum(gcol[h], jnp.max(log_d[h], axis=1, keepdims=True)) for h in hd]
    inter = [jnp.exp(gcol[h] - m_t[h]) for h in hd]
    qk = [_bdot_nt(q_all[:, s], k_all[:, s]) for s in sls]
    nd = [_bdot(jnp.concatenate([inter[h] * q_all[:, sls[h]], qk[h] * jnp.exp(log_d[h] - m_t[h])], axis=1),
                jnp.concatenate([state[h], vaug[h]], axis=0)) for h in hd]
    log_in = [b_end[h] - bc[h] + li_c[:, h:h + 1] for h in hd]
    m_new = [jnp.maximum(b_end[h] + m[h], jnp.max(log_in[h], axis=0, keepdims=True)) for h in hd]
    upd = [_bdot_tn(k_all[:, sls[h]] * jnp.exp(log_in[h] - m_new[h]), vaug[h]) for h in hd]
    for h in hd:
        h_ref[0, :, sls[h]] = nd[h][:, :dh] / jnp.maximum(jnp.abs(nd[h][:, dh:dh + 1]), jnp.exp(-m_t[h]))
        c_ref[h] = jnp.exp(b_end[h] + m[h] - m_new[h]) * state[h] + upd[h]
        m_ref[h:h + 1, :] = jnp.broadcast_to(m_new[h], (1, LANE))


def _mlstm_scan(q, k, v, li, lf, n_ctx):
    B, L, W = k.shape
    G = 2 * B
    H = W // HEAD_DIM
    nc = L // ML_CHUNK
    gcol = jnp.concatenate([li, lf], axis=-1)
    grow = jnp.swapaxes(gcol.reshape(G, nc, ML_CHUNK, 2 * H), 2, 3)
    cmap = _seq_chunk_map(B, n_ctx, nc)

    def shared(g, c):
        return (g % B,) + cmap(g, c)[1:]

    return pl.pallas_call(
        functools.partial(_mlstm_kernel, n_heads=H, n_fwd=B),
        grid=(G, nc),
        in_specs=[pl.BlockSpec((1, ML_CHUNK, W), shared)] * 3 + [
                  pl.BlockSpec((1, ML_CHUNK, 2 * H), cmap),
                  pl.BlockSpec((1, 1, 2 * H, ML_CHUNK), lambda g, c: cmap(g, c) + (0,))],
        out_specs=pl.BlockSpec((1, ML_CHUNK, W), cmap),
        out_shape=jax.ShapeDtypeStruct((G, L, W), f32),
        scratch_shapes=[pltpu.VMEM((H, HEAD_DIM, 2 * HEAD_DIM), f32),
                        pltpu.VMEM((H, LANE), f32)],
        compiler_params=pltpu.CompilerParams(dimension_semantics=("parallel", "arbitrary"),
                                             vmem_limit_bytes=VMEM_LIMIT),
        name="mlstm")(q, k, v, gcol, grow)


def _na_row_start(r, n_rows, win_r):
    return jnp.clip(r - win_r // 2, 0, n_rows - win_r)


def _na_kernel(qr_ref, qp_ref, k_ref, v_ref, kc_ref, vc_ref, bias_ref, o_ref, *, n_heads, n_rows, win_r):
    dh = HEAD_DIM
    r = pl.program_id(1)
    start = pl.multiple_of(_na_row_start(r, n_rows, win_r) * GRID_W, GRID_W)
    kwin = k_ref[0, pl.ds(start, win_r * GRID_W), :]
    vwin = v_ref[0, pl.ds(start, win_r * GRID_W), :]
    qr = qr_ref[0]
    qp = qp_ref[0]
    kc = kc_ref[0]
    vc = vc_ref[0]
    for h in range(n_heads):
        sl = slice(h * dh, (h + 1) * dh)
        s_loc = _bdot_nt(qr[:, sl], kwin[:, sl]) + bias_ref[h, 0]
        s_ctx = _bdot_nt(qp[:, sl], kc[:, sl])
        m = jnp.maximum(jnp.max(s_loc, axis=1, keepdims=True), jnp.max(s_ctx, axis=1, keepdims=True))
        p_loc = jnp.exp(s_loc - m)
        p_ctx = jnp.exp(s_ctx - m)
        den = jnp.sum(p_loc, axis=1, keepdims=True) + jnp.sum(p_ctx, axis=1, keepdims=True)
        o = _bdot(p_loc, vwin[:, sl]) + _bdot(p_ctx, vc[:, sl])
        o_ref[0, :, sl] = o / den


def _na_attention(q_rot, q_plain, k_rot, v, kc, vc, bias_tab, n_rows, win_r):
    B, S, W = q_rot.shape
    Lc = kc.shape[1]
    H = W // HEAD_DIM
    n_var = bias_tab.shape[1]
    qspec = pl.BlockSpec((1, GRID_W, W), lambda b, r: (b, r, 0))
    full = pl.BlockSpec((1, S, W), lambda b, r: (b, 0, 0))
    cspec = pl.BlockSpec((1, Lc, W), lambda b, r: (b, 0, 0))

    def bias_map(b, r):
        return (0, _na_row_start(r, n_rows, win_r) - r + win_r - 1, 0, 0)

    return pl.pallas_call(
        functools.partial(_na_kernel, n_heads=H, n_rows=n_rows, win_r=win_r),
        grid=(B, n_rows),
        in_specs=[qspec, qspec, full, full, cspec, cspec,
                  pl.BlockSpec((H, 1, GRID_W, win_r * GRID_W), bias_map)],
        out_specs=qspec,
        out_shape=jax.ShapeDtypeStruct((B, S, W), f32),
        compiler_params=pltpu.CompilerParams(dimension_semantics=("parallel", "arbitrary"),
                                             vmem_limit_bytes=VMEM_LIMIT),
        name="na")(q_rot, q_plain, k_rot, v, kc, vc, bias_tab)


def _ctx_attn_kernel(q_ref, k_ref, v_ref, o_ref, *, n_heads):
    dh = HEAD_DIM
    q = q_ref[0]
    k = k_ref[0]
    v = v_ref[0]
    for h in range(n_heads):
        sl = slice(h * dh, (h + 1) * dh)
        s = _bdot_nt(q[:, sl], k[:, sl])
        m = jnp.max(s, axis=1, keepdims=True)
        p = jnp.exp(s - m)
        o_ref[0, :, sl] = _bdot(p, v[:, sl]) / jnp.sum(p, axis=1, keepdims=True)


def _ctx_attention(q, k, v):
    B, Lc, W = q.shape
    spec = pl.BlockSpec((1, Lc, W), lambda b: (b, 0, 0))
    return pl.pallas_call(
        functools.partial(_ctx_attn_kernel, n_heads=W // HEAD_DIM),
        grid=(B,), in_specs=[spec] * 3, out_specs=spec,
        out_shape=jax.ShapeDtypeStruct((B, Lc, W), f32),
        compiler_params=pltpu.CompilerParams(dimension_semantics=("parallel",), vmem_limit_bytes=VMEM_LIMIT),
        name="ctx_attn")(q, k, v)


def _moe_kernel(be_ref, nused_ref, x_ref, g_ref, wgu_ref, bgu_ref, wdn_ref, bdn_ref, o_ref,
                wgu_scr, wdn_scr, *, d_expert):
    i = pl.program_id(0)

    @pl.when((i == 0) | (be_ref[i] != be_ref[jnp.maximum(i - 1, 0)]))
    def _():
        wgu_scr[...] = wgu_ref[0].astype(bf16)
        wdn_scr[...] = wdn_ref[0].astype(bf16)

    @pl.when(i < nused_ref[0])
    def _():
        gu = jnp.dot(x_ref[...], wgu_scr[...], preferred_element_type=f32) + bgu_ref[0]
        gt = jnp.minimum(gu[:, :d_expert], SWIGLU_LIMIT)
        up = jnp.clip(gu[:, d_expert:], -SWIGLU_LIMIT, SWIGLU_LIMIT)
        act = (up + 1.0) * (gt / (1.0 + jnp.exp(-SWIGLU_ALPHA * gt)))
        y = jnp.dot(act.astype(bf16), wdn_scr[...], preferred_element_type=f32) + bdn_ref[0]
        o_ref[...] = y * g_ref[...]

    @pl.when(i >= nused_ref[0])
    def _():
        o_ref[...] = jnp.zeros_like(o_ref)


def _moe_experts(xg, g_buf, block_e, n_used, w_gu, b_gu, w_dn, b_dn):
    R, D = xg.shape
    n_blocks = R // MOE_BLOCK
    E, _, F2 = w_gu.shape
    grid_spec = pltpu.PrefetchScalarGridSpec(
        num_scalar_prefetch=2,
        grid=(n_blocks,),
        in_specs=[pl.BlockSpec((MOE_BLOCK, D), lambda i, be, nu: (i, 0)),
                  pl.BlockSpec((MOE_BLOCK, 1), lambda i, be, nu: (i, 0)),
                  pl.BlockSpec((1, D, F2), lambda i, be, nu: (be[i], 0, 0)),
                  pl.BlockSpec((1, 1, F2), lambda i, be, nu: (be[i], 0, 0)),
                  pl.BlockSpec((1, F2 // 2, D), lambda i, be, nu: (be[i], 0, 0)),
                  pl.BlockSpec((1, 1, D), lambda i, be, nu: (be[i], 0, 0))],
        out_specs=pl.BlockSpec((MOE_BLOCK, D), lambda i, be, nu: (i, 0)),
        scratch_shapes=[pltpu.VMEM((D, F2), bf16), pltpu.VMEM((F2 // 2, D), bf16)])
    return pl.pallas_call(
        functools.partial(_moe_kernel, d_expert=F2 // 2),
        grid_spec=grid_spec,
        out_shape=jax.ShapeDtypeStruct((R, D), f32),
        compiler_params=pltpu.CompilerParams(dimension_semantics=("arbitrary",), vmem_limit_bytes=VMEM_LIMIT),
        name="moe")(block_e, n_used, xg, g_buf[:, None], w_gu, b_gu[:, None, :], w_dn, b_dn[:, None, :])


def _moe(h, w_router, b_router, w_gu, b_gu, w_dn, b_dn):
    T, D = h.shape
    E = w_router.shape[1]
    h_hi = h.astype(bf16)
    h_lo = (h - h_hi.astype(f32)).astype(bf16)
    w_hi = w_router.astype(bf16)
    w_lo = (w_router - w_hi.astype(f32)).astype(bf16)
    logits = _mm(h_hi, w_hi) + _mm(h_lo, w_hi) + _mm(h_hi, w_lo) + b_router
    top_val, top_idx = lax.top_k(logits, TOP_K)
    gate = jax.nn.softmax(top_val, axis=-1)
    flat_e = top_idx.reshape(-1)
    flat_g = gate.reshape(-1)
    order = jnp.argsort(flat_e)
    e_sorted = flat_e[order]
    counts = jnp.bincount(flat_e, length=E)
    padded = (counts + MOE_BLOCK - 1) // MOE_BLOCK * MOE_BLOCK
    pad_end = jnp.cumsum(padded)
    pad_start = pad_end - padded
    start = jnp.cumsum(counts) - counts
    dest = (pad_start[e_sorted] + jnp.arange(T * TOP_K, dtype=jnp.int32) - start[e_sorted]).astype(jnp.int32)
    n_blocks = (T * TOP_K + MOE_BLOCK - 1) // MOE_BLOCK + E
    tok_sorted = (order // TOP_K).astype(jnp.int32)
    tok_buf = jnp.zeros((n_blocks * MOE_BLOCK,), jnp.int32).at[dest].set(tok_sorted)
    g_buf = jnp.zeros((n_blocks * MOE_BLOCK,), f32).at[dest].set(flat_g[order])
    block_e = jnp.minimum(jnp.searchsorted(pad_end, jnp.arange(n_blocks) * MOE_BLOCK, side='right'),
                          E - 1).astype(jnp.int32)
    n_used = (pad_end[-1:] // MOE_BLOCK).astype(jnp.int32)
    xg = h_hi[tok_buf]
    ys = _moe_experts(xg, g_buf, block_e, n_used, w_gu, b_gu, w_dn, b_dn)
    pos = jnp.zeros((T * TOP_K,), jnp.int32).at[order].set(dest).reshape(T, TOP_K)
    out = ys[pos[:, 0]]
    for j in range(1, TOP_K):
        out = out + ys[pos[:, j]]
    return out


def _rms_norm(x, g):
    return x * lax.rsqrt(jnp.mean(x * x, -1, keepdims=True) + NORM_EPS) * g


def _group_norm(y, g, b, n_heads, eps):
    shp = y.shape
    yh = y.reshape(shp[:-1] + (n_heads, shp[-1] // n_heads))
    mu = jnp.mean(yh, -1, keepdims=True)
    var = jnp.mean(jnp.square(yh - mu), -1, keepdims=True)
    out = ((yh - mu) * lax.rsqrt(var + eps)).reshape(shp) * g
    return out if b is None else out + b


def _dft_mats(n, scale):
    j = jnp.arange(n, dtype=jnp.int32)
    m = (j[:, None] * j[None, :]) % n
    ang = m.astype(f32) * (2.0 * math.pi / n)
    return jnp.cos(ang) * scale, jnp.sin(ang) * scale


def _axial_rope(t, rows, cols):
    half = HEAD_DIM // 2
    nf = half // 2
    inv = ROPE_THETA ** (-jnp.arange(nf, dtype=f32) / nf)

    def rot(u, pos):
        ang = pos.astype(f32)[:, None] * inv
        cos = jnp.cos(ang)[None, :, None, :]
        sin = jnp.sin(ang)[None, :, None, :]
        u1, u2 = u[..., :nf], u[..., nf:]
        return jnp.concatenate([u1 * cos - u2 * sin, u1 * sin + u2 * cos], -1)

    return jnp.concatenate([rot(t[..., :half], rows), rot(t[..., half:], cols)], -1)


def _seg_map(fn, u, Lc):
    return jnp.concatenate([fn(u[:, :Lc]), fn(u[:, Lc:])], axis=1)


def _shift(u, direction):
    if direction == 0:
        return jnp.pad(u[:, :-1], ((0, 0), (1, 0), (0, 0)))
    return jnp.pad(u[:, 1:], ((0, 0), (0, 1), (0, 0)))


def _conv3(u, w):
    return w[0] * _shift(u, 0) + w[1] * u + w[2] * _shift(u, 1)


def _na_bias_table(rpb, win_r):
    H = rpb.shape[0]
    w = jnp.arange(GRID_W)
    col_start = jnp.clip(w - NA_WIN_COLS // 2, 0, GRID_W - NA_WIN_COLS)
    col_in = (w[None, :] >= col_start[:, None]) & (w[None, :] < col_start[:, None] + NA_WIN_COLS)
    col_off = jnp.clip(w[None, :] - w[:, None], 1 - NA_WIN_COLS, NA_WIN_COLS - 1) + NA_WIN_COLS - 1
    per_row = jnp.where(col_in[None, None], rpb[:, :, col_off], NEG_BIG)
    tabs = []
    for j in range(win_r):
        d0 = j - (win_r - 1)
        rows = [per_row[:, d0 + a + NA_WIN_ROWS - 1] for a in range(win_r)]
        tabs.append(jnp.concatenate(rows, axis=-1))
    return jnp.stack(tabs, axis=1)


def _mixer(hcat, lp, B, Lc, S, last):
    D = hcat.shape[-1]
    BW = D // N_BRANCHES
    H = BW // HEAD_DIM
    Lt = Lc + S
    M = B * Lt
    w_in = lp['w_in']
    names = ('na_k', 'na_v', 'rw_k', 'rw_v', 'rw_wl', 'rw_al', 'ml_k', 'ml_v', 'ml_g',
             'fo', 'na_q', 'rw_r', 'rw_gl', 'ml_q', 'ml_o', 'gates')
    sizes = (BW, BW, BW, BW, 2 * RW_DECAY_LORA, 2 * RW_ICL_LORA, BW, BW, 4 * H,
             BW, BW, BW, RW_GATE_LORA, BW, BW, N_BRANCHES * D)
    src, o = {}, 0
    for nm, sz in zip(names, sizes):
        src[nm] = (o, sz)
        o += sz
    groups = (('na_k', 'na_v', 'na_q'), ('rw_k', 'rw_v', 'rw_r', 'rw_wl', 'rw_al', 'rw_gl'),
              ('ml_k', 'ml_v', 'ml_q', 'ml_o', 'ml_g'), ('fo',))
    cols, dst, o = [], {}, 0
    for grp in groups:
        for nm in grp:
            a, sz = src[nm]
            cols.append(w_in[:, a:a + sz])
            dst[nm] = (o, sz)
            o += sz
        pad = (-o) % BW
        if pad:
            cols.append(jnp.zeros((D, pad), w_in.dtype))
            o += pad
    gate_off = o
    a, sz = src['gates']
    w2 = jnp.concatenate(cols + [w_in[:, a:a + sz]], axis=1)
    h2d = hcat.reshape(M, D).astype(bf16)
    px = _mm_w(h2d, w2, 0, gate_off).reshape(B, Lt, gate_off)

    def part(nm):
        a_, sz_ = dst[nm]
        return px[..., a_:a_ + sz_]

    na_k, na_v, na_q = part('na_k'), part('na_v'), part('na_q')
    rw_k, rw_v, rw_r = part('rw_k'), part('rw_v'), part('rw_r')
    rw_wl, rw_al, rw_gl = part('rw_wl'), part('rw_al'), part('rw_gl')
    ml_k, ml_v, ml_q, ml_o, ml_g = part('ml_k'), part('ml_v'), part('ml_q'), part('ml_o'), part('ml_g')
    fo = part('fo')

    C = BW // FOURIER_GROUPS
    cc, sc = _dft_mats(C, C ** -0.5)
    eye_g = jnp.eye(FOURIER_GROUPS, dtype=f32)
    chan = jnp.concatenate([jnp.kron(eye_g, cc), -jnp.kron(eye_g, sc)], axis=1)
    uc = _mm(fo.reshape(M, BW), chan).reshape(B, Lt, 2 * BW)

    def pos_dft(u, L):
        cl, sl_ = _dft_mats(L, L ** -0.5)
        rhs = jnp.concatenate([jnp.swapaxes(u[..., :BW], 0, 1).reshape(L, B * BW),
                               jnp.swapaxes(u[..., BW:], 0, 1).reshape(L, B * BW)], axis=0)
        y = _mm(jnp.concatenate([cl, sl_], axis=1), rhs)
        return jnp.swapaxes(y.reshape(L, B, BW), 0, 1)

    y_four = jnp.concatenate([pos_dft(uc[:, :Lc], Lc), pos_dft(uc[:, Lc:], S)], axis=1)

    n_rows = S // GRID_W
    win_r = min(NA_WIN_ROWS, n_rows)
    scale = HEAD_DIM ** -0.5
    t = jnp.arange(S)
    q_lat = na_q[:, Lc:].reshape(B, S, H, HEAD_DIM)
    k_lat = na_k[:, Lc:].reshape(B, S, H, HEAD_DIM)
    q_rot = (_axial_rope(q_lat, t // GRID_W, t % GRID_W) * scale).reshape(B, S, BW).astype(bf16)
    k_rot = _axial_rope(k_lat, t // GRID_W, t % GRID_W).reshape(B, S, BW).astype(bf16)
    q_plain = (na_q[:, Lc:] * scale).astype(bf16)
    kc = na_k[:, :Lc].astype(bf16)
    vc = na_v[:, :Lc].astype(bf16)
    bias_tab = _na_bias_table(lp['na_rpb'], win_r)
    y_na_x = _na_attention(q_rot, q_plain, k_rot, na_v[:, Lc:].astype(bf16), kc, vc, bias_tab, n_rows, win_r)
    if last:
        y_na_c = jnp.zeros((B, Lc, BW), f32)
    else:
        y_na_c = _ctx_attention((na_q[:, :Lc] * scale).astype(bf16), kc, vc)
    y_na = jnp.concatenate([y_na_c, y_na_x], axis=1)

    def flip_seg(u):
        return _seg_map(lambda s: jnp.flip(s, axis=1), u, Lc)

    seqs = []
    bonus_in = []
    for d in (0, 1):
        mu = lp['rw_mu_rkv'][d]
        mu_wa = lp['rw_mu_wa'][d]

        def lerp(u, m_):
            return u + m_ * (_seg_map(lambda s: _shift(s, d), u, Lc) - u)

        r_l = lerp(rw_r, mu[:BW])
        k_l = lerp(rw_k, mu[BW:2 * BW])
        v_l = lerp(rw_v, mu[2 * BW:])
        wl_l = lerp(rw_wl[..., d * RW_DECAY_LORA:(d + 1) * RW_DECAY_LORA], mu_wa[:RW_DECAY_LORA])
        al_l = lerp(rw_al[..., d * RW_ICL_LORA:(d + 1) * RW_ICL_LORA], mu_wa[RW_DECAY_LORA:])
        w_pre = lp['rw_w0'][d] + _mm_w(jnp.tanh(wl_l).reshape(M, -1), lp['rw_w_up'][d]).reshape(B, Lt, BW)
        w_log = -jax.nn.softplus(-w_pre) - 0.5
        lw = -jnp.exp(w_log)
        a_icl = jax.nn.sigmoid(lp['rw_a0'][d] + _mm_w(al_l.reshape(M, -1), lp['rw_a_up'][d]).reshape(B, Lt, BW))
        kk = (k_l * lp['rw_k_k']).reshape(B, Lt, H, HEAD_DIM)
        kk = (kk / jnp.maximum(jnp.sqrt(jnp.sum(kk * kk, -1, keepdims=True)), 1e-12)).reshape(B, Lt, BW)
        k_mod = k_l * (1.0 + (a_icl - 1.0) * lp['rw_k_a'])
        seqs.append((lw, k_mod, v_l, -kk, kk * a_icl, r_l))
        bonus_in.append((r_l, k_mod, v_l))
    stacked = [jnp.concatenate([seqs[0][i], seqs[1][i]], axis=0) for i in range(6)]
    ys = _rwkv_scan(*stacked, n_ctx=Lc // RW_CHUNK)
    y_rw = 0.0
    for d in (0, 1):
        y_d = _group_norm(ys[d * B:(d + 1) * B], lp['rw_gn_g'][d], lp['rw_gn_b'][d], H, RW_GN_EPS)
        r_l, k_mod, v_l = bonus_in[d]
        hs = (B, Lt, H, HEAD_DIM)
        bonus = jnp.sum(r_l.reshape(hs) * k_mod.reshape(hs) * lp['rw_r_k'][d], -1, keepdims=True) * v_l.reshape(hs)
        y_rw = y_rw + y_d + bonus.reshape(B, Lt, BW)
    y_rw = y_rw * _mm_w(jax.nn.sigmoid(rw_gl).reshape(M, -1), lp['rw_g_up']).reshape(B, Lt, BW)

    k_m = jax.nn.silu(_seg_map(lambda s: _conv3(s, lp['ml_conv_k']), ml_k, Lc)) * scale
    q_m = jax.nn.silu(_seg_map(lambda s: _conv3(s, lp['ml_conv_q']), ml_q, Lc))
    g4 = ml_g.reshape(B, Lt, 4, H) + lp['ml_gate_b']
    li = g4[:, :, :2]
    lf = jax.nn.log_sigmoid(g4[:, :, 2:])
    li_s = jnp.concatenate([li[:, :, 0], li[:, :, 1]], axis=0)
    lf_s = jnp.concatenate([lf[:, :, 0], lf[:, :, 1]], axis=0)
    hs_ = _mlstm_scan(q_m, k_m, ml_v, li_s, lf_s, n_ctx=Lc // ML_CHUNK)
    h_sum = hs_[:B] + hs_[B:]
    y_ml = _group_norm(h_sum, lp['ml_gn_g'], None, H, GN_EPS) * jax.nn.sigmoid(ml_o)

    ys4 = [u.reshape(M, BW).astype(bf16) for u in (y_four, y_na, y_rw, y_ml)]
    acc = _merge(h2d, ys4, w2, gate_off, lp['w_branch'])
    return _mm_w(acc, lp['w_out']).reshape(B, Lt, D)


def kernel(x, c, ctx, c_ctx, w_ada, b_ada, g_norm1, g_norm2, w_in, na_rpb, rw_mu_rkv, rw_mu_wa, rw_w_up, rw_w0, rw_a_up, rw_a0, rw_g_up, rw_k_k, rw_k_a, rw_r_k, rw_gn_g, rw_gn_b, ml_conv_q, ml_conv_k, ml_gate_b, ml_gn_g, w_branch, w_out, w_router, b_router, w_gu, b_gu, w_dn, b_dn, g_final):
    B, S, D = x.shape
    Lc = ctx.shape[1]
    depth = w_ada.shape[0]
    mod_in = jnp.concatenate([jax.nn.silu(c), jax.nn.silu(c_ctx)[None]], axis=0)
    mod_in = jnp.pad(mod_in, ((0, (-(B + 1)) % 16), (0, 0)))
    for l in range(depth):
        last = l == depth - 1
        lp = {'w_in': w_in[l], 'na_rpb': na_rpb[l], 'rw_mu_rkv': rw_mu_rkv[l], 'rw_mu_wa': rw_mu_wa[l],
              'rw_w_up': rw_w_up[l], 'rw_w0': rw_w0[l], 'rw_a_up': rw_a_up[l], 'rw_a0': rw_a0[l],
              'rw_g_up': rw_g_up[l], 'rw_k_k': rw_k_k[l], 'rw_k_a': rw_k_a[l], 'rw_r_k': rw_r_k[l],
              'rw_gn_g': rw_gn_g[l], 'rw_gn_b': rw_gn_b[l], 'ml_conv_q': ml_conv_q[l],
              'ml_conv_k': ml_conv_k[l], 'ml_gate_b': ml_gate_b[l], 'ml_gn_g': ml_gn_g[l],
              'w_branch': w_branch[l], 'w_out': w_out[l]}
        mod = _mm_w(mod_in, w_ada[l]) + b_ada[l]
        mx = jnp.split(mod[:B], 6, axis=-1)
        mc = jnp.split(mod[B], 6, axis=-1)
        hx = _rms_norm(x, g_norm1[l]) * (1.0 + mx[1][:, None]) + mx[0][:, None]
        hc = _rms_norm(ctx, g_norm1[l]) * (1.0 + mc[1]) + mc[0]
        y = _mixer(jnp.concatenate([hc, hx], axis=1), lp, B, Lc, S, last)
        x = x + mx[2][:, None] * y[:, Lc:]
        hx2 = _rms_norm(x, g_norm2[l]) * (1.0 + mx[4][:, None]) + mx[3][:, None]
        if last:
            ym = _moe(hx2.reshape(-1, D), w_router[l], b_router[l], w_gu[l], b_gu[l], w_dn[l], b_dn[l])
            x = x + mx[5][:, None] * ym.reshape(x.shape)
        else:
            ctx = ctx + mc[2] * y[:, :Lc]
            hc2 = _rms_norm(ctx, g_norm2[l]) * (1.0 + mc[4]) + mc[3]
            n_x = B * S
            ym = _moe(jnp.concatenate([hx2.reshape(-1, D), hc2.reshape(-1, D)], axis=0),
                      w_router[l], b_router[l], w_gu[l], b_gu[l], w_dn[l], b_dn[l])
            x = x + mx[5][:, None] * ym[:n_x].reshape(x.shape)
            ctx = ctx + mc[5] * ym[n_x:].reshape(ctx.shape)
    return _rms_norm(x, g_final)
```

```python
import functools
import math

import jax
import jax.numpy as jnp
from jax import lax
from jax.experimental import pallas as pl
from jax.experimental.pallas import tpu as pltpu

f32 = jnp.float32
bf16 = jnp.bfloat16

HEAD_DIM = 64
GRID_W = 64
N_BRANCHES = 4
FOURIER_GROUPS = 4
NA_WIN_ROWS = 8
NA_WIN_COLS = 16
ROPE_THETA = 10000.0
RW_DECAY_LORA = 64
RW_ICL_LORA = 64
RW_GATE_LORA = 128
RW_GN_EPS = 64e-5
ML_CHUNK = 64
RW_CHUNK = 64
TOP_K = 4
SWIGLU_LIMIT = 7.0
SWIGLU_ALPHA = 1.702
MOE_BLOCK = 256
NORM_EPS = 1e-6
GN_EPS = 1e-5
NEG_BIG = -1e30

LANE = 128
VMEM_LIMIT = 56 * 1024 * 1024


def _bdot(a, b):
    return jnp.dot(a.astype(bf16), b.astype(bf16), preferred_element_type=f32)


def _bdot_nt(a, b):
    return lax.dot_general(a.astype(bf16), b.astype(bf16), (((1,), (1,)), ((), ())),
                           preferred_element_type=f32)


def _bdot_tn(a, b):
    return lax.dot_general(a.astype(bf16), b.astype(bf16), (((0,), (0,)), ((), ())),
                           preferred_element_type=f32)


def _split_dot(tri, x):
    hi = x.astype(bf16)
    lo = (x - hi.astype(f32)).astype(bf16)
    return (jnp.dot(tri, hi, preferred_element_type=f32) + jnp.dot(tri, lo, preferred_element_type=f32))


def _split_dot_r(x, tri):
    hi = x.astype(bf16)
    lo = (x - hi.astype(f32)).astype(bf16)
    return (jnp.dot(hi, tri, preferred_element_type=f32) + jnp.dot(lo, tri, preferred_element_type=f32))


def _pick(dim, cap, align):
    if dim <= cap:
        return dim
    t = cap - cap % align
    while t >= align:
        if dim % t == 0:
            return t
        t -= align
    return dim


def _mm_kernel_single(a_ref, b_ref, o_ref):
    o_ref[...] = jnp.dot(a_ref[...], b_ref[...], preferred_element_type=f32).astype(o_ref.dtype)


def _mm_kernel_acc(a_ref, b_ref, o_ref, acc_ref, *, nk):
    k = pl.program_id(2)

    @pl.when(k == 0)
    def _():
        acc_ref[...] = jnp.zeros_like(acc_ref)

    acc_ref[...] += jnp.dot(a_ref[...], b_ref[...], preferred_element_type=f32)

    @pl.when(k == nk - 1)
    def _():
        o_ref[...] = acc_ref[...].astype(o_ref.dtype)


def _mm(a, b, out_dtype=f32):
    a = a.astype(bf16)
    b = b.astype(bf16)
    M, K = a.shape
    N = b.shape[1]
    tm = _pick(M, 1024, 16)
    tn = _pick(N, 1024, LANE)
    tk = _pick(K, 2048, LANE)
    nk = K // tk
    grid = (M // tm, N // tn, nk)
    in_specs = [pl.BlockSpec((tm, tk), lambda i, j, k: (i, k)),
                pl.BlockSpec((tk, tn), lambda i, j, k: (k, j))]
    out_spec = pl.BlockSpec((tm, tn), lambda i, j, k: (i, j))
    params = pltpu.CompilerParams(dimension_semantics=("parallel", "parallel", "arbitrary"),
                                  vmem_limit_bytes=VMEM_LIMIT)
    if nk == 1:
        return pl.pallas_call(_mm_kernel_single, grid=grid, in_specs=in_specs, out_specs=out_spec,
                              out_shape=jax.ShapeDtypeStruct((M, N), out_dtype),
                              compiler_params=params, name="mm")(a, b)
    return pl.pallas_call(functools.partial(_mm_kernel_acc, nk=nk), grid=grid, in_specs=in_specs,
                          out_specs=out_spec, out_shape=jax.ShapeDtypeStruct((M, N), out_dtype),
                          scratch_shapes=[pltpu.VMEM((tm, tn), f32)],
                          compiler_params=params, name="mm_acc")(a, b)


def _mm_ws_kernel(a_ref, b_ref, o_ref, w_ref):
    @pl.when(pl.program_id(1) == 0)
    def _():
        w_ref[...] = b_ref[...].astype(bf16)

    o_ref[...] = jnp.dot(a_ref[...], w_ref[...], preferred_element_type=f32).astype(o_ref.dtype)


def _mm_w(a, w, col_off=0, n_cols=None, out_dtype=f32, lead=()):
    a = a.astype(bf16)
    M, K = a.shape
    n_cols = w.shape[-1] - col_off if n_cols is None else n_cols
    tm = _pick(M, 1024, 16)
    tn = _pick(n_cols, 512, LANE)
    if col_off % tn:
        tn = math.gcd(tn, col_off)
    off = col_off // tn
    return pl.pallas_call(
        _mm_ws_kernel,
        grid=(n_cols // tn, M // tm),
        in_specs=[pl.BlockSpec((tm, K), lambda j, i: (i, 0)),
                  pl.BlockSpec((None,) * len(lead) + (K, tn), lambda j, i: tuple(lead) + (0, j + off))],
        out_specs=pl.BlockSpec((tm, tn), lambda j, i: (i, j)),
        out_shape=jax.ShapeDtypeStruct((M, n_cols), out_dtype),
        scratch_shapes=[pltpu.VMEM((K, tn), bf16)],
        compiler_params=pltpu.CompilerParams(dimension_semantics=("parallel", "arbitrary"),
                                             vmem_limit_bytes=VMEM_LIMIT),
        name="mm_w")(a, w)


def _merge_kernel(*refs, nb):
    h_ref = refs[0]
    y_refs, g_refs = refs[1:1 + nb], refs[1 + nb:1 + 2 * nb]
    wb_ref, o_ref, wg_scr, wb_scr = refs[1 + 2 * nb:]

    @pl.when(pl.program_id(1) == 0)
    def _():
        for i in range(nb):
            wg_scr[i] = g_refs[i][...].astype(bf16)
        wb_scr[...] = wb_ref[...].astype(bf16)

    h = h_ref[...]
    acc = None
    for i in range(nb):
        gate = jnp.dot(h, wg_scr[i], preferred_element_type=f32)
        p = jnp.dot(y_refs[i][...], wb_scr[i], preferred_element_type=f32)
        t = p / (1.0 + jnp.exp(-gate))
        acc = t if acc is None else acc + t
    o_ref[...] = acc.astype(o_ref.dtype)


def _merge(h, ys, w_all, gate_off, w_branch, layer):
    nb = len(ys)
    M, K = h.shape
    W = ys[0].shape[1]
    D = w_branch.shape[-1]
    tm = _pick(M, 1024, 16)
    tn = math.gcd(_pick(D, 256, LANE), gate_off)
    nj = D // tn
    off = gate_off // tn
    g_specs = [pl.BlockSpec((K, tn), functools.partial(lambda j, i, b: (0, off + b * nj + j), b=b))
               for b in range(nb)]
    return pl.pallas_call(
        functools.partial(_merge_kernel, nb=nb),
        grid=(nj, M // tm),
        in_specs=[pl.BlockSpec((tm, K), lambda j, i: (i, 0))]
                 + [pl.BlockSpec((tm, W), lambda j, i: (i, 0)) for _ in range(nb)]
                 + g_specs + [pl.BlockSpec((None, nb, W, tn), lambda j, i: (layer, 0, 0, j))],
        out_specs=pl.BlockSpec((tm, tn), lambda j, i: (i, j)),
        out_shape=jax.ShapeDtypeStruct((M, D), bf16),
        scratch_shapes=[pltpu.VMEM((nb, K, tn), bf16), pltpu.VMEM((nb, W, tn), bf16)],
        compiler_params=pltpu.CompilerParams(dimension_semantics=("parallel", "arbitrary"),
                                             vmem_limit_bytes=VMEM_LIMIT),
        name="merge")(h, *ys, *([w_all] * nb), w_branch)


def _rwkv_kernel(*refs, n_heads, n_fwd):
    T = RW_CHUNK
    dh = HEAD_DIM
    hd = range(n_heads)
    y_ref, h_ref = refs[12], refs[13]

    @pl.when(pl.program_id(1) == 0)
    def _():
        h_ref[...] = jnp.zeros_like(h_ref)

    rev = pl.program_id(0) >= n_fwd
    row = lax.broadcasted_iota(jnp.int32, (T, T), 0)
    col = lax.broadcasted_iota(jnp.int32, (T, T), 1)
    sign = jnp.where(rev, -1, 1)
    strict = row * sign > col * sign
    eye = row == col
    incl = strict | eye
    tri = jnp.where(incl, 1.0, 0.0).astype(bf16)

    lw, kk, v_all, aa, bb, rr = [jnp.where(rev, rb[0], rf[0]) for rf, rb in zip(refs[:6], refs[6:12])]
    c = _split_dot(tri, lw)
    c_end = jnp.where(rev, c[0:1, :], c[T - 1:T, :])
    e_inc = jnp.exp(c)
    e_exc = jnp.exp(c - lw)
    e_neg = jnp.exp(-c)
    e_rem = jnp.exp(c_end - c)
    g_end = jnp.exp(c_end)
    at_all = aa * e_exc
    rt_all = rr * e_inc
    bt_all = bb * e_neg
    kt_all = kk * e_neg
    bp_all = bb * e_rem
    kp_all = kk * e_rem
    sls = [slice(h * dh, (h + 1) * dh) for h in hd]
    zeros = jnp.zeros((T, dh), f32)

    g = [_bdot_nt(jnp.concatenate([at_all[:, s], rt_all[:, s]], axis=0),
                  jnp.concatenate([bt_all[:, s], kt_all[:, s]], axis=0)) for s in sls]
    p = [jnp.where(strict, g[h][:T, :T], 0.0) for h in hd]
    u = [_bdot(jnp.where(strict, g[h][:T, T:], 0.0), v_all[:, sls[h]]) for h in hd]
    x = [jnp.concatenate([at_all[:, sls[h]], u[h]], axis=1) for h in hd]
    n = 1
    while 2 * n < T:
        px = [_bdot(p[h], jnp.concatenate([x[h], p[h]], axis=1)) for h in hd]
        x = [x[h] + px[h][:, :2 * dh] for h in hd]
        p = [px[h][:, 2 * dh:] for h in hd]
        n *= 2
    x = [x[h] + _bdot(p[h], x[h]) for h in hd]
    z = [jnp.concatenate([x[h], jnp.concatenate([zeros, v_all[:, sls[h]]], axis=1)], axis=0) for h in hd]
    mx = [_bdot(jnp.concatenate([jnp.where(incl, g[h][T:, :T], 0.0), jnp.where(incl, g[h][T:, T:], 0.0)], axis=1),
                z[h]) for h in hd]
    bx = [_bdot_tn(jnp.concatenate([bp_all[:, sls[h]], kp_all[:, sls[h]]], axis=0), z[h]) for h in hd]
    yh = [_bdot(jnp.concatenate([rt_all[:, sls[h]] + mx[h][:, :dh],
                                 jnp.where(eye, g_end[:, sls[h]], 0.0) + bx[h][:, :dh]], axis=0), h_ref[h])
          for h in hd]
    for h in hd:
        y_ref[0, :, sls[h]] = yh[h][:T] + mx[h][:, dh:]
        h_ref[h] = yh[h][T:] + bx[h][:, dh:]


def _seq_chunk_map(n_fwd, n_ctx, nc):
    def index_map(g, c):
        back = jnp.where(c < n_ctx, n_ctx - 1 - c, nc - 1 - c + n_ctx)
        return (g, jnp.where(g >= n_fwd, back, c), 0)
    return index_map


def _rwkv_scan(fwd, bwd, n_ctx):
    B, L, W = fwd[0].shape
    G = 2 * B
    n_heads = W // HEAD_DIM
    cmap = _seq_chunk_map(B, n_ctx, L // RW_CHUNK)

    def fwd_map(g, c):
        on = g < B
        return (jnp.where(on, g, 0), jnp.where(on, c, 0), 0)

    def bwd_map(g, c):
        on = g >= B
        return (jnp.where(on, g - B, 0), jnp.where(on, cmap(g, c)[1], 0), 0)

    blk = (1, RW_CHUNK, W)
    return pl.pallas_call(
        functools.partial(_rwkv_kernel, n_heads=n_heads, n_fwd=B),
        grid=(G, L // RW_CHUNK),
        in_specs=[pl.BlockSpec(blk, fwd_map)] * 6 + [pl.BlockSpec(blk, bwd_map)] * 6,
        out_specs=pl.BlockSpec(blk, cmap),
        out_shape=jax.ShapeDtypeStruct((G, L, W), f32),
        scratch_shapes=[pltpu.VMEM((n_heads, HEAD_DIM, HEAD_DIM), f32)],
        compiler_params=pltpu.CompilerParams(dimension_semantics=("parallel", "arbitrary"),
                                             vmem_limit_bytes=VMEM_LIMIT),
        name="rwkv")(*fwd, *bwd)


def _mlstm_kernel(q_ref, k_ref, v_ref, gc_ref, gr_ref, h_ref, c_ref, m_ref, *, n_heads, n_fwd):
    T = ML_CHUNK
    dh = HEAD_DIM
    hd = range(n_heads)

    @pl.when(pl.program_id(1) == 0)
    def _():
        c_ref[...] = jnp.zeros_like(c_ref)
        m_ref[...] = jnp.zeros_like(m_ref)

    rev = pl.program_id(0) >= n_fwd
    sign = jnp.where(rev, -1, 1)
    row = lax.broadcasted_iota(jnp.int32, (T, T), 0)
    col = lax.broadcasted_iota(jnp.int32, (T, T), 1)
    lower = row * sign >= col * sign
    tri_l = jnp.where(lower, 1.0, 0.0).astype(bf16)
    tri_u = jnp.where(col * sign >= row * sign, 1.0, 0.0).astype(bf16)

    gc = gc_ref[0]
    gr = gr_ref[0, 0]
    li_c = gc[:, :n_heads]
    b_c = _split_dot(tri_l, gc[:, n_heads:])
    li_r = gr[:n_heads, :]
    b_r = _split_dot_r(gr[n_heads:, :], tri_u)
    b_end_all = jnp.where(rev, b_c[0:1, :], b_c[T - 1:T, :])

    q_all = q_ref[0]
    k_all = k_ref[0]
    v_all = v_ref[0]
    sls = [slice(h * dh, (h + 1) * dh) for h in hd]
    one_col = jnp.where(lax.broadcasted_iota(jnp.int32, (T, dh), 1) == 0, 1.0, 0.0)
    vaug = [jnp.concatenate([v_all[:, s], one_col], axis=1) for s in sls]
    bc = [b_c[:, h:h + 1] for h in hd]
    m = [m_ref[h:h + 1, 0:1] for h in hd]
    b_end = [b_end_all[:, h:h + 1] for h in hd]
    state = [c_ref[h] for h in hd]

    gcol = [bc[h] + m[h] for h in hd]
    log_d = [jnp.where(lower, bc[h] - b_r[h:h + 1, :] + li_r[h:h + 1, :], NEG_BIG) for h in hd]
    m_t = [jnp.maximum(gcol[h], jnp.max(log_d[h], axis=1, keepdims=True)) for h in hd]
    inter = [jnp.exp(gcol[h] - m_t[h]) for h in hd]
    qk = [_bdot_nt(q_all[:, s], k_all[:, s]) for s in sls]
    nd = [_bdot(jnp.concatenate([inter[h] * q_all[:, sls[h]], qk[h] * jnp.exp(log_d[h] - m_t[h])], axis=1),
                jnp.concatenate([state[h], vaug[h]], axis=0)) for h in hd]
    log_in = [b_end[h] - bc[h] + li_c[:, h:h + 1] for h in hd]
    m_new = [jnp.maximum(b_end[h] + m[h], jnp.max(log_in[h], axis=0, keepdims=True)) for h in hd]
    upd = [_bdot_tn(k_all[:, sls[h]] * jnp.exp(log_in[h] - m_new[h]), vaug[h]) for h in hd]
    for h in hd:
        h_ref[0, :, sls[h]] = nd[h][:, :dh] / jnp.maximum(jnp.abs(nd[h][:, dh:dh + 1]), jnp.exp(-m_t[h]))
        c_ref[h] = jnp.exp(b_end[h] + m[h] - m_new[h]) * state[h] + upd[h]
        m_ref[h:h + 1, :] = jnp.broadcast_to(m_new[h], (1, LANE))


def _mlstm_scan(q, k, v, li, lf, n_ctx):
    B, L, W = k.shape
    G = 2 * B
    H = W // HEAD_DIM
    nc = L // ML_CHUNK
    gcol = jnp.concatenate([li, lf], axis=-1)
    grow = jnp.swapaxes(gcol.reshape(G, nc, ML_CHUNK, 2 * H), 2, 3)
    cmap = _seq_chunk_map(B, n_ctx, nc)

    def shared(g, c):
        return (g % B,) + cmap(g, c)[1:]

    return pl.pallas_call(
        functools.partial(_mlstm_kernel, n_heads=H, n_fwd=B),
        grid=(G, nc),
        in_specs=[pl.BlockSpec((1, ML_CHUNK, W), shared)] * 3 + [
                  pl.BlockSpec((1, ML_CHUNK, 2 * H), cmap),
                  pl.BlockSpec((1, 1, 2 * H, ML_CHUNK), lambda g, c: cmap(g, c) + (0,))],
        out_specs=pl.BlockSpec((1, ML_CHUNK, W), cmap),
        out_shape=jax.ShapeDtypeStruct((G, L, W), f32),
        scratch_shapes=[pltpu.VMEM((H, HEAD_DIM, 2 * HEAD_DIM), f32),
                        pltpu.VMEM((H, LANE), f32)],
        compiler_params=pltpu.CompilerParams(dimension_semantics=("parallel", "arbitrary"),
                                             vmem_limit_bytes=VMEM_LIMIT),
        name="mlstm")(q, k, v, gcol, grow)


def _na_row_start(r, n_rows, win_r):
    return jnp.clip(r - win_r // 2, 0, n_rows - win_r)


def _na_kernel(qr_ref, qp_ref, k_ref, v_ref, kc_ref, vc_ref, bias_ref, o_ref, *, n_heads, n_rows, win_r):
    dh = HEAD_DIM
    r = pl.program_id(1)
    start = pl.multiple_of(_na_row_start(r, n_rows, win_r) * GRID_W, GRID_W)
    kwin = k_ref[0, pl.ds(start, win_r * GRID_W), :]
    vwin = v_ref[0, pl.ds(start, win_r * GRID_W), :]
    qr = qr_ref[0]
    qp = qp_ref[0]
    kc = kc_ref[0]
    vc = vc_ref[0]
    hd = range(n_heads)
    sls = [slice(h * dh, (h + 1) * dh) for h in hd]
    s_loc = [_bdot_nt(qr[:, sls[h]], kwin[:, sls[h]]) + bias_ref[h, 0] for h in hd]
    s_ctx = [_bdot_nt(qp[:, s], kc[:, s]) for s in sls]
    m = [jnp.maximum(jnp.max(s_loc[h], axis=1, keepdims=True), jnp.max(s_ctx[h], axis=1, keepdims=True)) for h in hd]
    p = [jnp.concatenate([jnp.exp(s_loc[h] - m[h]), jnp.exp(s_ctx[h] - m[h])], axis=1) for h in hd]
    o = [_bdot(p[h], jnp.concatenate([vwin[:, sls[h]], vc[:, sls[h]]], axis=0)) for h in hd]
    for h in hd:
        o_ref[0, :, sls[h]] = o[h] / jnp.sum(p[h], axis=1, keepdims=True)


def _na_attention(q_rot, q_plain, k_rot, v, kc, vc, bias_tab, n_rows, win_r):
    B, S, W = q_rot.shape
    Lc = kc.shape[1]
    H = W // HEAD_DIM
    n_var = bias_tab.shape[1]
    qspec = pl.BlockSpec((1, GRID_W, W), lambda b, r: (b, r, 0))
    full = pl.BlockSpec((1, S, W), lambda b, r: (b, 0, 0))
    cspec = pl.BlockSpec((1, Lc, W), lambda b, r: (b, 0, 0))

    def bias_map(b, r):
        return (0, _na_row_start(r, n_rows, win_r) - r + win_r - 1, 0, 0)

    return pl.pallas_call(
        functools.partial(_na_kernel, n_heads=H, n_rows=n_rows, win_r=win_r),
        grid=(B, n_rows),
        in_specs=[qspec, qspec, full, full, cspec, cspec,
                  pl.BlockSpec((H, 1, GRID_W, win_r * GRID_W), bias_map)],
        out_specs=qspec,
        out_shape=jax.ShapeDtypeStruct((B, S, W), f32),
        compiler_params=pltpu.CompilerParams(dimension_semantics=("parallel", "arbitrary"),
                                             vmem_limit_bytes=VMEM_LIMIT),
        name="na")(q_rot, q_plain, k_rot, v, kc, vc, bias_tab)


def _ctx_attn_kernel(q_ref, k_ref, v_ref, o_ref, *, n_heads):
    dh = HEAD_DIM
    q = q_ref[0]
    k = k_ref[0]
    v = v_ref[0]
    for h in range(n_heads):
        sl = slice(h * dh, (h + 1) * dh)
        s = _bdot_nt(q[:, sl], k[:, sl])
        m = jnp.max(s, axis=1, keepdims=True)
        p = jnp.exp(s - m)
        o_ref[0, :, sl] = _bdot(p, v[:, sl]) / jnp.sum(p, axis=1, keepdims=True)


def _ctx_attention(q, k, v):
    B, Lc, W = q.shape
    spec = pl.BlockSpec((1, Lc, W), lambda b: (b, 0, 0))
    return pl.pallas_call(
        functools.partial(_ctx_attn_kernel, n_heads=W // HEAD_DIM),
        grid=(B,), in_specs=[spec] * 3, out_specs=spec,
        out_shape=jax.ShapeDtypeStruct((B, Lc, W), f32),
        compiler_params=pltpu.CompilerParams(dimension_semantics=("parallel",), vmem_limit_bytes=VMEM_LIMIT),
        name="ctx_attn")(q, k, v)


def _moe_kernel(be_ref, nused_ref, x_ref, wgu_ref, bgu_ref, wdn_ref, bdn_ref, o_ref,
                wgu_scr, wdn_scr, *, d_expert):
    i = pl.program_id(0)

    @pl.when((i == 0) | (be_ref[i] != be_ref[jnp.maximum(i - 1, 0)]))
    def _():
        wgu_scr[...] = wgu_ref[0].astype(bf16)
        wdn_scr[...] = wdn_ref[0].astype(bf16)

    @pl.when(i < nused_ref[0])
    def _():
        gu = jnp.dot(x_ref[...], wgu_scr[...], preferred_element_type=f32) + bgu_ref[0]
        gt = jnp.minimum(gu[:, :d_expert], SWIGLU_LIMIT)
        up = jnp.clip(gu[:, d_expert:], -SWIGLU_LIMIT, SWIGLU_LIMIT)
        act = (up + 1.0) * (gt / (1.0 + jnp.exp(-SWIGLU_ALPHA * gt)))
        y = jnp.dot(act.astype(bf16), wdn_scr[...], preferred_element_type=f32) + bdn_ref[0]
        o_ref[...] = y.astype(o_ref.dtype)

    @pl.when(i >= nused_ref[0])
    def _():
        o_ref[...] = jnp.zeros_like(o_ref)


def _moe_experts(xg, block_e, n_used, w_gu, b_gu, w_dn, b_dn, layer):
    R, D = xg.shape
    n_blocks = R // MOE_BLOCK
    F2 = w_gu.shape[-1]
    grid_spec = pltpu.PrefetchScalarGridSpec(
        num_scalar_prefetch=2,
        grid=(n_blocks,),
        in_specs=[pl.BlockSpec((MOE_BLOCK, D), lambda i, be, nu: (i, 0)),
                  pl.BlockSpec((None, 1, D, F2), lambda i, be, nu: (layer, be[i], 0, 0)),
                  pl.BlockSpec((None, 1, 1, F2), lambda i, be, nu: (layer, be[i], 0, 0)),
                  pl.BlockSpec((None, 1, F2 // 2, D), lambda i, be, nu: (layer, be[i], 0, 0)),
                  pl.BlockSpec((None, 1, 1, D), lambda i, be, nu: (layer, be[i], 0, 0))],
        out_specs=pl.BlockSpec((MOE_BLOCK, D), lambda i, be, nu: (i, 0)),
        scratch_shapes=[pltpu.VMEM((D, F2), bf16), pltpu.VMEM((F2 // 2, D), bf16)])
    return pl.pallas_call(
        functools.partial(_moe_kernel, d_expert=F2 // 2),
        grid_spec=grid_spec,
        out_shape=jax.ShapeDtypeStruct((R, D), bf16),
        compiler_params=pltpu.CompilerParams(dimension_semantics=("arbitrary",), vmem_limit_bytes=VMEM_LIMIT),
        name="moe")(block_e, n_used, xg, w_gu, b_gu[:, :, None, :], w_dn, b_dn[:, :, None, :])


def _moe(h, w_router, b_router, w_gu, b_gu, w_dn, b_dn, layer):
    T, D = h.shape
    E = w_router.shape[1]
    A = T * TOP_K
    h_hi = h.astype(bf16)
    h_lo = (h - h_hi.astype(f32)).astype(bf16)
    w_hi = w_router.astype(bf16)
    w_lo = (w_router - w_hi.astype(f32)).astype(bf16)
    logits = _mm(h_hi, w_hi) + _mm(h_lo, w_hi) + _mm(h_hi, w_lo) + b_router
    top_val, top_idx = lax.top_k(logits, TOP_K)
    gate = jax.nn.softmax(top_val, axis=-1)
    flat_e = top_idx.reshape(-1).astype(jnp.int32)
    onehot = flat_e[:, None] == jnp.arange(E, dtype=jnp.int32)
    counts = jnp.sum(onehot.astype(jnp.int32), axis=0)
    padded = (counts + MOE_BLOCK - 1) // MOE_BLOCK * MOE_BLOCK
    pad_end = jnp.cumsum(padded)
    pad_start = pad_end - padded
    start = jnp.cumsum(counts) - counts
    order = jnp.argsort(flat_e).astype(jnp.int32)
    inv = jnp.argsort(order).astype(jnp.int32)
    shift_e = pad_start - start
    pos = (inv + jnp.sum(jnp.where(onehot, shift_e, 0), axis=1)).reshape(T, TOP_K)
    n_blocks = (A + MOE_BLOCK - 1) // MOE_BLOCK + E
    blk_first = jnp.arange(n_blocks, dtype=jnp.int32) * MOE_BLOCK
    block_e = jnp.minimum(jnp.sum((pad_end[None, :] <= blk_first[:, None]).astype(jnp.int32), axis=1), E - 1)
    n_used = (pad_end[-1:] // MOE_BLOCK).astype(jnp.int32)
    slot = jnp.arange(n_blocks * MOE_BLOCK, dtype=jnp.int32).reshape(n_blocks, MOE_BLOCK)
    off_in = slot - pad_start[block_e][:, None]
    valid = off_in < counts[block_e][:, None]
    srt = jnp.clip(start[block_e][:, None] + off_in, 0, A - 1)
    tok_buf = jnp.where(valid, order[srt.reshape(-1)].reshape(srt.shape) // TOP_K, 0).reshape(-1)
    xg = h_hi[tok_buf]
    ys = _moe_experts(xg, block_e, n_used, w_gu, b_gu, w_dn, b_dn, layer)
    out = gate[:, 0:1] * ys[pos[:, 0]]
    for j in range(1, TOP_K):
        out = out + gate[:, j:j + 1] * ys[pos[:, j]]
    return out


def _rms_norm(x, g):
    return x * lax.rsqrt(jnp.mean(x * x, -1, keepdims=True) + NORM_EPS) * g


def _group_norm(y, g, b, n_heads, eps):
    shp = y.shape
    yh = y.reshape(shp[:-1] + (n_heads, shp[-1] // n_heads))
    mu = jnp.mean(yh, -1, keepdims=True)
    var = jnp.mean(jnp.square(yh - mu), -1, keepdims=True)
    out = ((yh - mu) * lax.rsqrt(var + eps)).reshape(shp) * g
    return out if b is None else out + b


def _dft_mats(n, scale):
    j = jnp.arange(n, dtype=jnp.int32)
    m = (j[:, None] * j[None, :]) % n
    ang = m.astype(f32) * (2.0 * math.pi / n)
    return jnp.cos(ang) * scale, jnp.sin(ang) * scale


def _axial_rope(t, rows, cols):
    half = HEAD_DIM // 2
    nf = half // 2
    inv = ROPE_THETA ** (-jnp.arange(nf, dtype=f32) / nf)

    def rot(u, pos):
        ang = pos.astype(f32)[:, None] * inv
        cos = jnp.cos(ang)[None, :, None, :]
        sin = jnp.sin(ang)[None, :, None, :]
        u1, u2 = u[..., :nf], u[..., nf:]
        return jnp.concatenate([u1 * cos - u2 * sin, u1 * sin + u2 * cos], -1)

    return jnp.concatenate([rot(t[..., :half], rows), rot(t[..., half:], cols)], -1)


def _seg_map(fn, u, Lc):
    return jnp.concatenate([fn(u[:, :Lc]), fn(u[:, Lc:])], axis=1)


def _shift(u, direction):
    if direction == 0:
        return jnp.pad(u[:, :-1], ((0, 0), (1, 0), (0, 0)))
    return jnp.pad(u[:, 1:], ((0, 0), (0, 1), (0, 0)))


def _conv3(u, w):
    return w[0] * _shift(u, 0) + w[1] * u + w[2] * _shift(u, 1)


def _na_bias_table(rpb, win_r):
    H = rpb.shape[0]
    w = jnp.arange(GRID_W)
    col_start = jnp.clip(w - NA_WIN_COLS // 2, 0, GRID_W - NA_WIN_COLS)
    col_in = (w[None, :] >= col_start[:, None]) & (w[None, :] < col_start[:, None] + NA_WIN_COLS)
    col_off = jnp.clip(w[None, :] - w[:, None], 1 - NA_WIN_COLS, NA_WIN_COLS - 1) + NA_WIN_COLS - 1
    per_row = jnp.where(col_in[None, None], rpb[:, :, col_off], NEG_BIG)
    tabs = []
    for j in range(win_r):
        d0 = j - (win_r - 1)
        rows = [per_row[:, d0 + a + NA_WIN_ROWS - 1] for a in range(win_r)]
        tabs.append(jnp.concatenate(rows, axis=-1))
    return jnp.stack(tabs, axis=1)


def _mixer(hcat, lp, B, Lc, S, last):
    D = hcat.shape[-1]
    BW = D // N_BRANCHES
    H = BW // HEAD_DIM
    Lt = Lc + S
    M = B * Lt
    w_in = lp['w_in']
    names = ('na_k', 'na_v', 'rw_k', 'rw_v', 'rw_wl', 'rw_al', 'ml_k', 'ml_v', 'ml_g',
             'fo', 'na_q', 'rw_r', 'rw_gl', 'ml_q', 'ml_o', 'gates')
    sizes = (BW, BW, BW, BW, 2 * RW_DECAY_LORA, 2 * RW_ICL_LORA, BW, BW, 4 * H,
             BW, BW, BW, RW_GATE_LORA, BW, BW, N_BRANCHES * D)
    src, o = {}, 0
    for nm, sz in zip(names, sizes):
        src[nm] = (o, sz)
        o += sz
    groups = (('na_k', 'na_v', 'na_q'), ('rw_k', 'rw_v', 'rw_r', 'rw_wl', 'rw_al', 'rw_gl'),
              ('ml_k', 'ml_v', 'ml_q', 'ml_o', 'ml_g'), ('fo',))
    cols, dst, o = [], {}, 0
    for grp in groups:
        for nm in grp:
            a, sz = src[nm]
            cols.append(w_in[:, a:a + sz])
            dst[nm] = (o, sz)
            o += sz
        pad = (-o) % BW
        if pad:
            cols.append(jnp.zeros((D, pad), w_in.dtype))
            o += pad
    gate_off = o
    a, sz = src['gates']
    w2 = jnp.concatenate(cols + [w_in[:, a:a + sz]], axis=1)
    h2d = hcat.reshape(M, D).astype(bf16)
    px = _mm_w(h2d, w2, 0, gate_off).reshape(B, Lt, gate_off)

    def part(nm):
        a_, sz_ = dst[nm]
        return px[..., a_:a_ + sz_]

    na_k, na_v, na_q = part('na_k'), part('na_v'), part('na_q')
    rw_k, rw_v, rw_r = part('rw_k'), part('rw_v'), part('rw_r')
    rw_wl, rw_al, rw_gl = part('rw_wl'), part('rw_al'), part('rw_gl')
    ml_k, ml_v, ml_q, ml_o, ml_g = part('ml_k'), part('ml_v'), part('ml_q'), part('ml_o'), part('ml_g')
    fo = part('fo')

    C = BW // FOURIER_GROUPS
    cc, sc = _dft_mats(C, C ** -0.5)
    eye_g = jnp.eye(FOURIER_GROUPS, dtype=f32)
    chan = jnp.concatenate([jnp.kron(eye_g, cc), -jnp.kron(eye_g, sc)], axis=1)
    uc = _mm(fo.reshape(M, BW), chan).reshape(B, Lt, 2 * BW)

    def pos_dft(u, L):
        cl, sl_ = _dft_mats(L, L ** -0.5)
        rhs = jnp.concatenate([jnp.swapaxes(u[..., :BW], 0, 1).reshape(L, B * BW),
                               jnp.swapaxes(u[..., BW:], 0, 1).reshape(L, B * BW)], axis=0)
        y = _mm(jnp.concatenate([cl, sl_], axis=1), rhs)
        return jnp.swapaxes(y.reshape(L, B, BW), 0, 1)

    y_four = jnp.concatenate([pos_dft(uc[:, :Lc], Lc), pos_dft(uc[:, Lc:], S)], axis=1)

    n_rows = S // GRID_W
    win_r = min(NA_WIN_ROWS, n_rows)
    scale = HEAD_DIM ** -0.5
    t = jnp.arange(S)
    q_lat = na_q[:, Lc:].reshape(B, S, H, HEAD_DIM)
    k_lat = na_k[:, Lc:].reshape(B, S, H, HEAD_DIM)
    q_rot = (_axial_rope(q_lat, t // GRID_W, t % GRID_W) * scale).reshape(B, S, BW).astype(bf16)
    k_rot = _axial_rope(k_lat, t // GRID_W, t % GRID_W).reshape(B, S, BW).astype(bf16)
    q_plain = (na_q[:, Lc:] * scale).astype(bf16)
    kc = na_k[:, :Lc].astype(bf16)
    vc = na_v[:, :Lc].astype(bf16)
    bias_tab = _na_bias_table(lp['na_rpb'], win_r)
    y_na_x = _na_attention(q_rot, q_plain, k_rot, na_v[:, Lc:].astype(bf16), kc, vc, bias_tab, n_rows, win_r)
    if last:
        y_na_c = jnp.zeros((B, Lc, BW), f32)
    else:
        y_na_c = _ctx_attention((na_q[:, :Lc] * scale).astype(bf16), kc, vc)
    y_na = jnp.concatenate([y_na_c, y_na_x], axis=1)

    def flip_seg(u):
        return _seg_map(lambda s: jnp.flip(s, axis=1), u, Lc)

    seqs = []
    bonus_in = []
    for d in (0, 1):
        mu = lp['rw_mu_rkv'][d]
        mu_wa = lp['rw_mu_wa'][d]

        def lerp(u, m_):
            return u + m_ * (_seg_map(lambda s: _shift(s, d), u, Lc) - u)

        r_l = lerp(rw_r, mu[:BW])
        k_l = lerp(rw_k, mu[BW:2 * BW])
        v_l = lerp(rw_v, mu[2 * BW:])
        wl_l = lerp(rw_wl[..., d * RW_DECAY_LORA:(d + 1) * RW_DECAY_LORA], mu_wa[:RW_DECAY_LORA])
        al_l = lerp(rw_al[..., d * RW_ICL_LORA:(d + 1) * RW_ICL_LORA], mu_wa[RW_DECAY_LORA:])
        w_pre = lp['rw_w0'][d] + _mm_w(jnp.tanh(wl_l).reshape(M, -1), lp['rw_w_up'][d]).reshape(B, Lt, BW)
        w_log = -jax.nn.softplus(-w_pre) - 0.5
        lw = -jnp.exp(w_log)
        a_icl = jax.nn.sigmoid(lp['rw_a0'][d] + _mm_w(al_l.reshape(M, -1), lp['rw_a_up'][d]).reshape(B, Lt, BW))
        kk = (k_l * lp['rw_k_k']).reshape(B, Lt, H, HEAD_DIM)
        kk = (kk / jnp.maximum(jnp.sqrt(jnp.sum(kk * kk, -1, keepdims=True)), 1e-12)).reshape(B, Lt, BW)
        k_mod = k_l * (1.0 + (a_icl - 1.0) * lp['rw_k_a'])
        seqs.append((lw, k_mod, v_l, -kk, kk * a_icl, r_l))
        bonus_in.append((r_l, k_mod, v_l))
    ys = _rwkv_scan(seqs[0], seqs[1], n_ctx=Lc // RW_CHUNK)
    y_rw = 0.0
    for d in (0, 1):
        y_d = _group_norm(ys[d * B:(d + 1) * B], lp['rw_gn_g'][d], lp['rw_gn_b'][d], H, RW_GN_EPS)
        r_l, k_mod, v_l = bonus_in[d]
        hs = (B, Lt, H, HEAD_DIM)
        bonus = jnp.sum(r_l.reshape(hs) * k_mod.reshape(hs) * lp['rw_r_k'][d], -1, keepdims=True) * v_l.reshape(hs)
        y_rw = y_rw + y_d + bonus.reshape(B, Lt, BW)
    y_rw = y_rw * _mm_w(jax.nn.sigmoid(rw_gl).reshape(M, -1), lp['rw_g_up']).reshape(B, Lt, BW)

    k_m = jax.nn.silu(_seg_map(lambda s: _conv3(s, lp['ml_conv_k']), ml_k, Lc)) * scale
    q_m = jax.nn.silu(_seg_map(lambda s: _conv3(s, lp['ml_conv_q']), ml_q, Lc))
    g4 = ml_g.reshape(B, Lt, 4, H) + lp['ml_gate_b']
    li = g4[:, :, :2]
    lf = jax.nn.log_sigmoid(g4[:, :, 2:])
    li_s = jnp.concatenate([li[:, :, 0], li[:, :, 1]], axis=0)
    lf_s = jnp.concatenate([lf[:, :, 0], lf[:, :, 1]], axis=0)
    hs_ = _mlstm_scan(q_m, k_m, ml_v, li_s, lf_s, n_ctx=Lc // ML_CHUNK)
    h_sum = hs_[:B] + hs_[B:]
    y_ml = _group_norm(h_sum, lp['ml_gn_g'], None, H, GN_EPS) * jax.nn.sigmoid(ml_o)

    ys4 = [u.reshape(M, BW).astype(bf16) for u in (y_four, y_na, y_rw, y_ml)]
    acc = _merge(h2d, ys4, w2, gate_off, lp['w_branch'], lp['layer'])
    return _mm_w(acc, lp['w_out'], lead=(lp['layer'],)).reshape(B, Lt, D)


def kernel(x, c, ctx, c_ctx, w_ada, b_ada, g_norm1, g_norm2, w_in, na_rpb, rw_mu_rkv, rw_mu_wa, rw_w_up, rw_w0, rw_a_up, rw_a0, rw_g_up, rw_k_k, rw_k_a, rw_r_k, rw_gn_g, rw_gn_b, ml_conv_q, ml_conv_k, ml_gate_b, ml_gn_g, w_branch, w_out, w_router, b_router, w_gu, b_gu, w_dn, b_dn, g_final):
    B, S, D = x.shape
    Lc = ctx.shape[1]
    depth = w_ada.shape[0]
    mod_in = jnp.concatenate([jax.nn.silu(c), jax.nn.silu(c_ctx)[None]], axis=0)
    mod_in = jnp.pad(mod_in, ((0, (-(B + 1)) % 16), (0, 0)))
    for l in range(depth):
        last = l == depth - 1
        lp = {'w_in': w_in[l], 'na_rpb': na_rpb[l], 'rw_mu_rkv': rw_mu_rkv[l], 'rw_mu_wa': rw_mu_wa[l],
              'rw_w_up': rw_w_up[l], 'rw_w0': rw_w0[l], 'rw_a_up': rw_a_up[l], 'rw_a0': rw_a0[l],
              'rw_g_up': rw_g_up[l], 'rw_k_k': rw_k_k[l], 'rw_k_a': rw_k_a[l], 'rw_r_k': rw_r_k[l],
              'rw_gn_g': rw_gn_g[l], 'rw_gn_b': rw_gn_b[l], 'ml_conv_q': ml_conv_q[l],
              'ml_conv_k': ml_conv_k[l], 'ml_gate_b': ml_gate_b[l], 'ml_gn_g': ml_gn_g[l],
              'w_branch': w_branch, 'w_out': w_out, 'layer': l}
        mod = _mm_w(mod_in, w_ada, lead=(l,)) + b_ada[l]
        mx = jnp.split(mod[:B], 6, axis=-1)
        mc = jnp.split(mod[B], 6, axis=-1)
        hx = _rms_norm(x, g_norm1[l]) * (1.0 + mx[1][:, None]) + mx[0][:, None]
        hc = _rms_norm(ctx, g_norm1[l]) * (1.0 + mc[1]) + mc[0]
        y = _mixer(jnp.concatenate([hc, hx], axis=1), lp, B, Lc, S, last)
        x = x + mx[2][:, None] * y[:, Lc:]
        hx2 = _rms_norm(x, g_norm2[l]) * (1.0 + mx[4][:, None]) + mx[3][:, None]
        if last:
            ym = _moe(hx2.reshape(-1, D), w_router[l], b_router[l], w_gu, b_gu, w_dn, b_dn, l)
            x = x + mx[5][:, None] * ym.reshape(x.shape)
        else:
            ctx = ctx + mc[2] * y[:, :Lc]
            hc2 = _rms_norm(ctx, g_norm2[l]) * (1.0 + mc[4]) + mc[3]
            n_x = B * S
            ym = _moe(jnp.concatenate([hx2.reshape(-1, D), hc2.reshape(-1, D)], axis=0),
                      w_router[l], b_router[l], w_gu, b_gu, w_dn, b_dn, l)
            x = x + mx[5][:, None] * ym[:n_x].reshape(x.shape)
            ctx = ctx + mc[5] * ym[n_x:].reshape(ctx.shape)
    return _rms_norm(x, g_final)
```

```python
import functools
import math

import jax
import jax.numpy as jnp
from jax import lax
from jax.experimental import pallas as pl
from jax.experimental.pallas import tpu as pltpu

f32 = jnp.float32
bf16 = jnp.bfloat16

HEAD_DIM = 64
GRID_W = 64
N_BRANCHES = 4
FOURIER_GROUPS = 4
NA_WIN_ROWS = 8
NA_WIN_COLS = 16
ROPE_THETA = 10000.0
RW_DECAY_LORA = 64
RW_ICL_LORA = 64
RW_GATE_LORA = 128
RW_GN_EPS = 64e-5
ML_CHUNK = 64
RW_CHUNK = 64
TOP_K = 4
SWIGLU_LIMIT = 7.0
SWIGLU_ALPHA = 1.702
MOE_BLOCK = 256
MOE_SPLIT = 4
NORM_EPS = 1e-6
GN_EPS = 1e-5
NEG_BIG = -1e30

LANE = 128
VMEM_LIMIT = 56 * 1024 * 1024


def _bdot(a, b):
    return jnp.dot(a.astype(bf16), b.astype(bf16), preferred_element_type=f32)


def _bdot_nt(a, b):
    return lax.dot_general(a.astype(bf16), b.astype(bf16), (((1,), (1,)), ((), ())),
                           preferred_element_type=f32)


def _bdot_tn(a, b):
    return lax.dot_general(a.astype(bf16), b.astype(bf16), (((0,), (0,)), ((), ())),
                           preferred_element_type=f32)


def _split_dot(tri, x):
    hi = x.astype(bf16)
    lo = (x - hi.astype(f32)).astype(bf16)
    return (jnp.dot(tri, hi, preferred_element_type=f32) + jnp.dot(tri, lo, preferred_element_type=f32))


def _split_dot_r(x, tri):
    hi = x.astype(bf16)
    lo = (x - hi.astype(f32)).astype(bf16)
    return (jnp.dot(hi, tri, preferred_element_type=f32) + jnp.dot(lo, tri, preferred_element_type=f32))


def _pick(dim, cap, align):
    if dim <= cap:
        return dim
    t = cap - cap % align
    while t >= align:
        if dim % t == 0:
            return t
        t -= align
    return dim


def _mm_kernel_single(a_ref, b_ref, o_ref):
    o_ref[...] = jnp.dot(a_ref[...], b_ref[...], preferred_element_type=f32).astype(o_ref.dtype)


def _mm_kernel_acc(a_ref, b_ref, o_ref, acc_ref, *, nk):
    k = pl.program_id(2)

    @pl.when(k == 0)
    def _():
        acc_ref[...] = jnp.zeros_like(acc_ref)

    acc_ref[...] += jnp.dot(a_ref[...], b_ref[...], preferred_element_type=f32)

    @pl.when(k == nk - 1)
    def _():
        o_ref[...] = acc_ref[...].astype(o_ref.dtype)


def _mm(a, b, out_dtype=f32):
    a = a.astype(bf16)
    b = b.astype(bf16)
    M, K = a.shape
    N = b.shape[1]
    tm = _pick(M, 1024, 16)
    tn = _pick(N, 1024, LANE)
    tk = _pick(K, 2048, LANE)
    nk = K // tk
    grid = (M // tm, N // tn, nk)
    in_specs = [pl.BlockSpec((tm, tk), lambda i, j, k: (i, k)),
                pl.BlockSpec((tk, tn), lambda i, j, k: (k, j))]
    out_spec = pl.BlockSpec((tm, tn), lambda i, j, k: (i, j))
    params = pltpu.CompilerParams(dimension_semantics=("parallel", "parallel", "arbitrary"),
                                  vmem_limit_bytes=VMEM_LIMIT)
    if nk == 1:
        return pl.pallas_call(_mm_kernel_single, grid=grid, in_specs=in_specs, out_specs=out_spec,
                              out_shape=jax.ShapeDtypeStruct((M, N), out_dtype),
                              compiler_params=params, name="mm")(a, b)
    return pl.pallas_call(functools.partial(_mm_kernel_acc, nk=nk), grid=grid, in_specs=in_specs,
                          out_specs=out_spec, out_shape=jax.ShapeDtypeStruct((M, N), out_dtype),
                          scratch_shapes=[pltpu.VMEM((tm, tn), f32)],
                          compiler_params=params, name="mm_acc")(a, b)


def _mm_ws_kernel(a_ref, b_ref, o_ref, w_ref):
    @pl.when(pl.program_id(1) == 0)
    def _():
        w_ref[...] = b_ref[...].astype(bf16)

    o_ref[...] = jnp.dot(a_ref[...], w_ref[...], preferred_element_type=f32).astype(o_ref.dtype)


def _mm_w(a, w, col_off=0, n_cols=None, out_dtype=f32, lead=()):
    a = a.astype(bf16)
    M, K = a.shape
    n_cols = w.shape[-1] - col_off if n_cols is None else n_cols
    tm = _pick(M, 1024, 16)
    tn = _pick(n_cols, 512, LANE)
    if col_off % tn:
        tn = math.gcd(tn, col_off)
    off = col_off // tn
    return pl.pallas_call(
        _mm_ws_kernel,
        grid=(n_cols // tn, M // tm),
        in_specs=[pl.BlockSpec((tm, K), lambda j, i: (i, 0)),
                  pl.BlockSpec((None,) * len(lead) + (K, tn), lambda j, i: tuple(lead) + (0, j + off))],
        out_specs=pl.BlockSpec((tm, tn), lambda j, i: (i, j)),
        out_shape=jax.ShapeDtypeStruct((M, n_cols), out_dtype),
        scratch_shapes=[pltpu.VMEM((K, tn), bf16)],
        compiler_params=pltpu.CompilerParams(dimension_semantics=("parallel", "arbitrary"),
                                             vmem_limit_bytes=VMEM_LIMIT),
        name="mm_w")(a, w)


def _merge_kernel(*refs, nb):
    h_ref = refs[0]
    y_refs, g_refs = refs[1:1 + nb], refs[1 + nb:1 + 2 * nb]
    wb_ref, o_ref, wg_scr, wb_scr = refs[1 + 2 * nb:]

    @pl.when(pl.program_id(1) == 0)
    def _():
        for i in range(nb):
            wg_scr[i] = g_refs[i][...].astype(bf16)
        wb_scr[...] = wb_ref[...].astype(bf16)

    h = h_ref[...]
    acc = None
    for i in range(nb):
        gate = jnp.dot(h, wg_scr[i], preferred_element_type=f32)
        p = jnp.dot(y_refs[i][...], wb_scr[i], preferred_element_type=f32)
        t = p / (1.0 + jnp.exp(-gate))
        acc = t if acc is None else acc + t
    o_ref[...] = acc.astype(o_ref.dtype)


def _merge(h, ys, w_all, gate_off, w_branch, layer):
    nb = len(ys)
    M, K = h.shape
    W = ys[0].shape[1]
    D = w_branch.shape[-1]
    tm = _pick(M, 1024, 16)
    tn = math.gcd(_pick(D, 256, LANE), gate_off)
    nj = D // tn
    off = gate_off // tn
    g_specs = [pl.BlockSpec((K, tn), functools.partial(lambda j, i, b: (0, off + b * nj + j), b=b))
               for b in range(nb)]
    return pl.pallas_call(
        functools.partial(_merge_kernel, nb=nb),
        grid=(nj, M // tm),
        in_specs=[pl.BlockSpec((tm, K), lambda j, i: (i, 0))]
                 + [pl.BlockSpec((tm, W), lambda j, i: (i, 0)) for _ in range(nb)]
                 + g_specs + [pl.BlockSpec((None, nb, W, tn), lambda j, i: (layer, 0, 0, j))],
        out_specs=pl.BlockSpec((tm, tn), lambda j, i: (i, j)),
        out_shape=jax.ShapeDtypeStruct((M, D), bf16),
        scratch_shapes=[pltpu.VMEM((nb, K, tn), bf16), pltpu.VMEM((nb, W, tn), bf16)],
        compiler_params=pltpu.CompilerParams(dimension_semantics=("parallel", "arbitrary"),
                                             vmem_limit_bytes=VMEM_LIMIT),
        name="merge")(h, *ys, *([w_all] * nb), w_branch)


def _rwkv_kernel(*refs, n_heads):
    T = RW_CHUNK
    dh = HEAD_DIM
    y_refs, h_ref = refs[12:14], refs[14]

    @pl.when(pl.program_id(1) == 0)
    def _():
        h_ref[...] = jnp.zeros_like(h_ref)

    row = lax.broadcasted_iota(jnp.int32, (T, T), 0)
    col = lax.broadcasted_iota(jnp.int32, (T, T), 1)
    eye = row == col
    zeros = jnp.zeros((T, dh), f32)
    chains = []
    for d in (0, 1):
        strict = row > col if d == 0 else row < col
        incl = strict | eye
        tri = jnp.where(incl, 1.0, 0.0).astype(bf16)
        lw, kk, v_all, aa, bb, rr = [r[0] for r in refs[6 * d:6 * d + 6]]
        c = _split_dot(tri, lw)
        c_end = c[T - 1:T, :] if d == 0 else c[0:1, :]
        e_neg = jnp.exp(-c)
        e_rem = jnp.exp(c_end - c)
        g_end = jnp.exp(c_end)
        at_all = aa * jnp.exp(c - lw)
        rt_all = rr * jnp.exp(c)
        bt_all = bb * e_neg
        kt_all = kk * e_neg
        bp_all = bb * e_rem
        kp_all = kk * e_rem
        for h in range(n_heads):
            s = slice(h * dh, (h + 1) * dh)
            chains.append((at_all[:, s], rt_all[:, s], bt_all[:, s], kt_all[:, s], bp_all[:, s], kp_all[:, s],
                           v_all[:, s], g_end[:, s], strict, incl))
    AT, RT, BT, KT, BP, KP, V, GE, ST, IN = zip(*chains)
    hd = range(len(chains))

    g = [_bdot_nt(jnp.concatenate([AT[i], RT[i]], axis=0), jnp.concatenate([BT[i], KT[i]], axis=0)) for i in hd]
    p = [jnp.where(ST[i], g[i][:T, :T], 0.0) for i in hd]
    u = [_bdot(jnp.where(ST[i], g[i][:T, T:], 0.0), V[i]) for i in hd]
    x = [jnp.concatenate([AT[i], u[i]], axis=1) for i in hd]
    n = 1
    while 2 * n < T:
        px = [_bdot(p[i], jnp.concatenate([x[i], p[i]], axis=1)) for i in hd]
        x = [x[i] + px[i][:, :2 * dh] for i in hd]
        p = [px[i][:, 2 * dh:] for i in hd]
        n *= 2
    x = [x[i] + _bdot(p[i], x[i]) for i in hd]
    z = [jnp.concatenate([x[i], jnp.concatenate([zeros, V[i]], axis=1)], axis=0) for i in hd]
    mx = [_bdot(jnp.concatenate([jnp.where(IN[i], g[i][T:, :T], 0.0), jnp.where(IN[i], g[i][T:, T:], 0.0)], axis=1),
                z[i]) for i in hd]
    bx = [_bdot_tn(jnp.concatenate([BP[i], KP[i]], axis=0), z[i]) for i in hd]
    yh = [_bdot(jnp.concatenate([RT[i] + mx[i][:, :dh], jnp.where(eye, GE[i], 0.0) + bx[i][:, :dh]], axis=0),
                h_ref[i]) for i in hd]
    for i in hd:
        d, h = divmod(i, n_heads)
        y_refs[d][0, :, h * dh:(h + 1) * dh] = yh[i][:T] + mx[i][:, dh:]
        h_ref[i] = yh[i][T:] + bx[i][:, dh:]


def _back_chunk(c, n_ctx, nc):
    return jnp.where(c < n_ctx, n_ctx - 1 - c, nc - 1 - c + n_ctx)


def _rwkv_scan(fwd, bwd, n_ctx):
    B, L, W = fwd[0].shape
    n_heads = W // HEAD_DIM
    nc = L // RW_CHUNK
    blk = (1, RW_CHUNK, W)
    f_spec = pl.BlockSpec(blk, lambda b, c: (b, c, 0))
    b_spec = pl.BlockSpec(blk, lambda b, c: (b, _back_chunk(c, n_ctx, nc), 0))
    return pl.pallas_call(
        functools.partial(_rwkv_kernel, n_heads=n_heads),
        grid=(B, nc),
        in_specs=[f_spec] * 6 + [b_spec] * 6,
        out_specs=[f_spec, b_spec],
        out_shape=[jax.ShapeDtypeStruct((B, L, W), f32)] * 2,
        scratch_shapes=[pltpu.VMEM((2 * n_heads, HEAD_DIM, HEAD_DIM), f32)],
        compiler_params=pltpu.CompilerParams(dimension_semantics=("parallel", "arbitrary"),
                                             vmem_limit_bytes=VMEM_LIMIT),
        name="rwkv")(*fwd, *bwd)


def _mlstm_kernel(*refs, n_heads, dirs):
    T = ML_CHUNK
    dh = HEAD_DIM
    nd_ = len(dirs)
    h_refs, c_ref, m_ref = refs[5 * nd_:6 * nd_], refs[6 * nd_], refs[6 * nd_ + 1]

    @pl.when(pl.program_id(1) == 0)
    def _():
        c_ref[...] = jnp.zeros_like(c_ref)
        m_ref[...] = jnp.zeros_like(m_ref)

    row = lax.broadcasted_iota(jnp.int32, (T, T), 0)
    col = lax.broadcasted_iota(jnp.int32, (T, T), 1)
    one_col = jnp.where(lax.broadcasted_iota(jnp.int32, (T, dh), 1) == 0, 1.0, 0.0)
    chains = []
    for j, d in enumerate(dirs):
        q_ref, k_ref, v_ref, gc_ref, gr_ref = refs[5 * j:5 * j + 5]
        lower = row >= col if d == 0 else row <= col
        tri_l = jnp.where(lower, 1.0, 0.0).astype(bf16)
        tri_u = jnp.where(row <= col if d == 0 else row >= col, 1.0, 0.0).astype(bf16)
        gc = gc_ref[0]
        gr = gr_ref[0, 0]
        li_c = gc[:, :n_heads]
        b_c = _split_dot(tri_l, gc[:, n_heads:])
        li_r = gr[:n_heads, :]
        b_r = _split_dot_r(gr[n_heads:, :], tri_u)
        b_end_all = b_c[T - 1:T, :] if d == 0 else b_c[0:1, :]
        q_all, k_all, v_all = q_ref[0], k_ref[0], v_ref[0]
        for h in range(n_heads):
            s = slice(h * dh, (h + 1) * dh)
            chains.append((q_all[:, s], k_all[:, s], jnp.concatenate([v_all[:, s], one_col], axis=1),
                           b_c[:, h:h + 1], li_c[:, h:h + 1], b_r[h:h + 1, :], li_r[h:h + 1, :],
                           b_end_all[:, h:h + 1], lower))
    Q, K, VA, BC, LIC, BR, LIR, BE, LO = zip(*chains)
    hd = range(len(chains))
    m = [m_ref[i:i + 1, 0:1] for i in hd]
    state = [c_ref[i] for i in hd]

    gcol = [BC[i] + m[i] for i in hd]
    log_d = [jnp.where(LO[i], BC[i] - BR[i] + LIR[i], NEG_BIG) for i in hd]
    m_t = [jnp.maximum(gcol[i], jnp.max(log_d[i], axis=1, keepdims=True)) for i in hd]
    inter = [jnp.exp(gcol[i] - m_t[i]) for i in hd]
    qk = [_bdot_nt(Q[i], K[i]) for i in hd]
    nd = [_bdot(jnp.concatenate([inter[i] * Q[i], qk[i] * jnp.exp(log_d[i] - m_t[i])], axis=1),
                jnp.concatenate([state[i], VA[i]], axis=0)) for i in hd]
    log_in = [BE[i] - BC[i] + LIC[i] for i in hd]
    m_new = [jnp.maximum(BE[i] + m[i], jnp.max(log_in[i], axis=0, keepdims=True)) for i in hd]
    upd = [_bdot_tn(K[i] * jnp.exp(log_in[i] - m_new[i]), VA[i]) for i in hd]
    for i in hd:
        j, h = divmod(i, n_heads)
        h_refs[j][0, :, h * dh:(h + 1) * dh] = (nd[i][:, :dh]
                                                / jnp.maximum(jnp.abs(nd[i][:, dh:dh + 1]), jnp.exp(-m_t[i])))
        c_ref[i] = jnp.exp(BE[i] + m[i] - m_new[i]) * state[i] + upd[i]
        m_ref[i:i + 1, :] = jnp.broadcast_to(m_new[i], (1, LANE))


def _mlstm_scan(q, k, v, li, lf, n_ctx):
    B, L, W = k.shape
    H = W // HEAD_DIM
    nc = L // ML_CHUNK
    gcol = jnp.concatenate([li, lf], axis=-1)
    grow = jnp.swapaxes(gcol.reshape(2 * B, nc, ML_CHUNK, 2 * H), 2, 3)

    def specs(d):
        def chunk(c):
            return c if d == 0 else _back_chunk(c, n_ctx, nc)
        seq = pl.BlockSpec((1, ML_CHUNK, W), lambda b, c: (b, chunk(c), 0))
        return seq, [seq] * 3 + [pl.BlockSpec((1, ML_CHUNK, 2 * H), lambda b, c: (b + d * B, chunk(c), 0)),
                                 pl.BlockSpec((1, 1, 2 * H, ML_CHUNK), lambda b, c: (b + d * B, chunk(c), 0, 0))]

    outs = []
    for d in (0, 1):
        out_spec, in_specs = specs(d)
        outs.append(pl.pallas_call(
            functools.partial(_mlstm_kernel, n_heads=H, dirs=(d,)),
            grid=(B, nc),
            in_specs=in_specs,
            out_specs=[out_spec],
            out_shape=[jax.ShapeDtypeStruct((B, L, W), f32)],
            scratch_shapes=[pltpu.VMEM((H, HEAD_DIM, 2 * HEAD_DIM), f32),
                            pltpu.VMEM((H, LANE), f32)],
            compiler_params=pltpu.CompilerParams(dimension_semantics=("parallel", "arbitrary"),
                                                 vmem_limit_bytes=VMEM_LIMIT),
            name="mlstm")(q, k, v, gcol, grow)[0])
    return outs


def _na_row_start(r, n_rows, win_r):
    return jnp.clip(r - win_r // 2, 0, n_rows - win_r)


def _na_kernel(qr_ref, qp_ref, k_ref, v_ref, kc_ref, vc_ref, bias_ref, o_ref, *, n_heads, n_rows, win_r):
    dh = HEAD_DIM
    r = pl.program_id(1)
    start = pl.multiple_of(_na_row_start(r, n_rows, win_r) * GRID_W, GRID_W)
    kwin = k_ref[0, pl.ds(start, win_r * GRID_W), :]
    vwin = v_ref[0, pl.ds(start, win_r * GRID_W), :]
    qr = qr_ref[0]
    qp = qp_ref[0]
    kc = kc_ref[0]
    vc = vc_ref[0]
    hd = range(n_heads)
    sls = [slice(h * dh, (h + 1) * dh) for h in hd]
    s_loc = [_bdot_nt(qr[:, sls[h]], kwin[:, sls[h]]) + bias_ref[h, 0] for h in hd]
    s_ctx = [_bdot_nt(qp[:, s], kc[:, s]) for s in sls]
    m = [jnp.maximum(jnp.max(s_loc[h], axis=1, keepdims=True), jnp.max(s_ctx[h], axis=1, keepdims=True)) for h in hd]
    p = [jnp.concatenate([jnp.exp(s_loc[h] - m[h]), jnp.exp(s_ctx[h] - m[h])], axis=1) for h in hd]
    o = [_bdot(p[h], jnp.concatenate([vwin[:, sls[h]], vc[:, sls[h]]], axis=0)) for h in hd]
    for h in hd:
        o_ref[0, :, sls[h]] = o[h] / jnp.sum(p[h], axis=1, keepdims=True)


def _na_attention(q_rot, q_plain, k_rot, v, kc, vc, bias_tab, n_rows, win_r):
    B, S, W = q_rot.shape
    Lc = kc.shape[1]
    H = W // HEAD_DIM
    n_var = bias_tab.shape[1]
    qspec = pl.BlockSpec((1, GRID_W, W), lambda b, r: (b, r, 0))
    full = pl.BlockSpec((1, S, W), lambda b, r: (b, 0, 0))
    cspec = pl.BlockSpec((1, Lc, W), lambda b, r: (b, 0, 0))

    def bias_map(b, r):
        return (0, _na_row_start(r, n_rows, win_r) - r + win_r - 1, 0, 0)

    return pl.pallas_call(
        functools.partial(_na_kernel, n_heads=H, n_rows=n_rows, win_r=win_r),
        grid=(B, n_rows),
        in_specs=[qspec, qspec, full, full, cspec, cspec,
                  pl.BlockSpec((H, 1, GRID_W, win_r * GRID_W), bias_map)],
        out_specs=qspec,
        out_shape=jax.ShapeDtypeStruct((B, S, W), f32),
        compiler_params=pltpu.CompilerParams(dimension_semantics=("parallel", "arbitrary"),
                                             vmem_limit_bytes=VMEM_LIMIT),
        name="na")(q_rot, q_plain, k_rot, v, kc, vc, bias_tab)


def _ctx_attn_kernel(q_ref, k_ref, v_ref, o_ref, *, n_heads):
    dh = HEAD_DIM
    q = q_ref[0]
    k = k_ref[0]
    v = v_ref[0]
    for h in range(n_heads):
        sl = slice(h * dh, (h + 1) * dh)
        s = _bdot_nt(q[:, sl], k[:, sl])
        m = jnp.max(s, axis=1, keepdims=True)
        p = jnp.exp(s - m)
        o_ref[0, :, sl] = _bdot(p, v[:, sl]) / jnp.sum(p, axis=1, keepdims=True)


def _ctx_attention(q, k, v):
    B, Lc, W = q.shape
    spec = pl.BlockSpec((1, Lc, W), lambda b: (b, 0, 0))
    return pl.pallas_call(
        functools.partial(_ctx_attn_kernel, n_heads=W // HEAD_DIM),
        grid=(B,), in_specs=[spec] * 3, out_specs=spec,
        out_shape=jax.ShapeDtypeStruct((B, Lc, W), f32),
        compiler_params=pltpu.CompilerParams(dimension_semantics=("parallel",), vmem_limit_bytes=VMEM_LIMIT),
        name="ctx_attn")(q, k, v)


def _moe_kernel(be_ref, nused_ref, x_ref, wgu_ref, bgu_ref, wdn_ref, bdn_ref, o_ref,
                wgu_scr, wdn_scr, *, d_expert):
    i = pl.program_id(0)

    @pl.when((i == 0) | (be_ref[i] != be_ref[jnp.maximum(i - 1, 0)]))
    def _():
        wgu_scr[...] = wgu_ref[0].astype(bf16)
        wdn_scr[...] = wdn_ref[0].astype(bf16)

    @pl.when(i < nused_ref[0])
    def _():
        gu = jnp.dot(x_ref[...], wgu_scr[...], preferred_element_type=f32) + bgu_ref[0]
        gt = jnp.minimum(gu[:, :d_expert], SWIGLU_LIMIT)
        up = jnp.clip(gu[:, d_expert:], -SWIGLU_LIMIT, SWIGLU_LIMIT)
        act = (up + 1.0) * (gt / (1.0 + jnp.exp(-SWIGLU_ALPHA * gt)))
        y = jnp.dot(act.astype(bf16), wdn_scr[...], preferred_element_type=f32) + bdn_ref[0]
        o_ref[...] = y.astype(o_ref.dtype)

    @pl.when(i >= nused_ref[0])
    def _():
        o_ref[...] = jnp.zeros_like(o_ref)


def _moe_experts(xg, block_e, n_used, w_gu, b_gu, w_dn, b_dn, layer):
    R, D = xg.shape
    n_blocks = R // MOE_BLOCK
    F2 = w_gu.shape[-1]
    grid_spec = pltpu.PrefetchScalarGridSpec(
        num_scalar_prefetch=2,
        grid=(n_blocks,),
        in_specs=[pl.BlockSpec((MOE_BLOCK, D), lambda i, be, nu: (i, 0)),
                  pl.BlockSpec((None, 1, D, F2), lambda i, be, nu: (layer, be[i], 0, 0)),
                  pl.BlockSpec((None, 1, 1, F2), lambda i, be, nu: (layer, be[i], 0, 0)),
                  pl.BlockSpec((None, 1, F2 // 2, D), lambda i, be, nu: (layer, be[i], 0, 0)),
                  pl.BlockSpec((None, 1, 1, D), lambda i, be, nu: (layer, be[i], 0, 0))],
        out_specs=pl.BlockSpec((MOE_BLOCK, D), lambda i, be, nu: (i, 0)),
        scratch_shapes=[pltpu.VMEM((D, F2), bf16), pltpu.VMEM((F2 // 2, D), bf16)])
    return pl.pallas_call(
        functools.partial(_moe_kernel, d_expert=F2 // 2),
        grid_spec=grid_spec,
        out_shape=jax.ShapeDtypeStruct((R, D), bf16),
        compiler_params=pltpu.CompilerParams(dimension_semantics=("arbitrary",), vmem_limit_bytes=VMEM_LIMIT),
        name="moe")(block_e, n_used, xg, w_gu, b_gu[:, :, None, :], w_dn, b_dn[:, :, None, :])


def _moe(h, w_router, b_router, w_gu, b_gu, w_dn, b_dn, layer):
    T, D = h.shape
    E = w_router.shape[1]
    A = T * TOP_K
    h_hi = h.astype(bf16)
    h_lo = (h - h_hi.astype(f32)).astype(bf16)
    w_hi = w_router.astype(bf16)
    w_lo = (w_router - w_hi.astype(f32)).astype(bf16)
    logits = _mm(h_hi, w_hi) + _mm(h_lo, w_hi) + _mm(h_hi, w_lo) + b_router
    top_val, top_idx = lax.top_k(logits, TOP_K)
    gate = jax.nn.softmax(top_val, axis=-1)
    flat_e = top_idx.reshape(-1).astype(jnp.int32)
    onehot = flat_e[:, None] == jnp.arange(E, dtype=jnp.int32)
    counts = jnp.sum(onehot.astype(jnp.int32), axis=0)
    padded = (counts + MOE_BLOCK - 1) // MOE_BLOCK * MOE_BLOCK
    pad_end = jnp.cumsum(padded)
    pad_start = pad_end - padded
    start = jnp.cumsum(counts) - counts
    order = jnp.argsort(flat_e).astype(jnp.int32)
    inv = jnp.argsort(order).astype(jnp.int32)
    shift_e = pad_start - start
    pos = (inv + jnp.sum(jnp.where(onehot, shift_e, 0), axis=1)).reshape(T, TOP_K)
    n_blocks = (A + MOE_BLOCK - 1) // MOE_BLOCK + E
    blk_first = jnp.arange(n_blocks, dtype=jnp.int32) * MOE_BLOCK
    block_e = jnp.minimum(jnp.sum((pad_end[None, :] <= blk_first[:, None]).astype(jnp.int32), axis=1), E - 1)
    n_used = (pad_end[-1:] // MOE_BLOCK).astype(jnp.int32)
    slot = jnp.arange(n_blocks * MOE_BLOCK, dtype=jnp.int32).reshape(n_blocks, MOE_BLOCK)
    off_in = slot - pad_start[block_e][:, None]
    valid = off_in < counts[block_e][:, None]
    srt = jnp.clip(start[block_e][:, None] + off_in, 0, A - 1)
    tok_buf = jnp.where(valid, order[srt.reshape(-1)].reshape(srt.shape) // TOP_K, 0).reshape(-1)
    parts = []
    per = -(-n_blocks // MOE_SPLIT)
    for q in range(MOE_SPLIT):
        lo, hi = q * per, min((q + 1) * per, n_blocks)
        if lo >= hi:
            break
        xg = h_hi[tok_buf[lo * MOE_BLOCK:hi * MOE_BLOCK]]
        parts.append(_moe_experts(xg, block_e[lo:hi], n_used - lo, w_gu, b_gu, w_dn, b_dn, layer))
    ys = jnp.concatenate(parts, axis=0)
    out = gate[:, 0:1] * ys[pos[:, 0]]
    for j in range(1, TOP_K):
        out = out + gate[:, j:j + 1] * ys[pos[:, j]]
    return out


def _rms_norm(x, g):
    return x * lax.rsqrt(jnp.mean(x * x, -1, keepdims=True) + NORM_EPS) * g


def _group_norm(y, g, b, n_heads, eps):
    shp = y.shape
    yh = y.reshape(shp[:-1] + (n_heads, shp[-1] // n_heads))
    mu = jnp.mean(yh, -1, keepdims=True)
    var = jnp.mean(jnp.square(yh - mu), -1, keepdims=True)
    out = ((yh - mu) * lax.rsqrt(var + eps)).reshape(shp) * g
    return out if b is None else out + b


def _dft_mats(n, scale):
    j = jnp.arange(n, dtype=jnp.int32)
    m = (j[:, None] * j[None, :]) % n
    ang = m.astype(f32) * (2.0 * math.pi / n)
    return jnp.cos(ang) * scale, jnp.sin(ang) * scale


def _axial_rope(t, rows, cols):
    half = HEAD_DIM // 2
    nf = half // 2
    inv = ROPE_THETA ** (-jnp.arange(nf, dtype=f32) / nf)

    def rot(u, pos):
        ang = pos.astype(f32)[:, None] * inv
        cos = jnp.cos(ang)[None, :, None, :]
        sin = jnp.sin(ang)[None, :, None, :]
        u1, u2 = u[..., :nf], u[..., nf:]
        return jnp.concatenate([u1 * cos - u2 * sin, u1 * sin + u2 * cos], -1)

    return jnp.concatenate([rot(t[..., :half], rows), rot(t[..., half:], cols)], -1)


def _seg_map(fn, u, Lc):
    return jnp.concatenate([fn(u[:, :Lc]), fn(u[:, Lc:])], axis=1)


def _shift(u, direction):
    if direction == 0:
        return jnp.pad(u[:, :-1], ((0, 0), (1, 0), (0, 0)))
    return jnp.pad(u[:, 1:], ((0, 0), (0, 1), (0, 0)))


def _conv3(u, w):
    return w[0] * _shift(u, 0) + w[1] * u + w[2] * _shift(u, 1)


def _na_bias_table(rpb, win_r):
    H = rpb.shape[0]
    w = jnp.arange(GRID_W)
    col_start = jnp.clip(w - NA_WIN_COLS // 2, 0, GRID_W - NA_WIN_COLS)
    col_in = (w[None, :] >= col_start[:, None]) & (w[None, :] < col_start[:, None] + NA_WIN_COLS)
    col_off = jnp.clip(w[None, :] - w[:, None], 1 - NA_WIN_COLS, NA_WIN_COLS - 1) + NA_WIN_COLS - 1
    per_row = jnp.where(col_in[None, None], rpb[:, :, col_off], NEG_BIG)
    tabs = []
    for j in range(win_r):
        d0 = j - (win_r - 1)
        rows = [per_row[:, d0 + a + NA_WIN_ROWS - 1] for a in range(win_r)]
        tabs.append(jnp.concatenate(rows, axis=-1))
    return jnp.stack(tabs, axis=1)


def _mixer(hcat, lp, B, Lc, S, last):
    D = hcat.shape[-1]
    BW = D // N_BRANCHES
    H = BW // HEAD_DIM
    Lt = Lc + S
    M = B * Lt
    w_in = lp['w_in']
    names = ('na_k', 'na_v', 'rw_k', 'rw_v', 'rw_wl', 'rw_al', 'ml_k', 'ml_v', 'ml_g',
             'fo', 'na_q', 'rw_r', 'rw_gl', 'ml_q', 'ml_o', 'gates')
    sizes = (BW, BW, BW, BW, 2 * RW_DECAY_LORA, 2 * RW_ICL_LORA, BW, BW, 4 * H,
             BW, BW, BW, RW_GATE_LORA, BW, BW, N_BRANCHES * D)
    src, o = {}, 0
    for nm, sz in zip(names, sizes):
        src[nm] = (o, sz)
        o += sz
    groups = (('na_k', 'na_v', 'na_q'), ('rw_k', 'rw_v', 'rw_r', 'rw_wl', 'rw_al', 'rw_gl'),
              ('ml_k', 'ml_v', 'ml_q', 'ml_o', 'ml_g'), ('fo',))
    cols, dst, o = [], {}, 0
    for grp in groups:
        for nm in grp:
            a, sz = src[nm]
            cols.append(w_in[:, a:a + sz])
            dst[nm] = (o, sz)
            o += sz
        pad = (-o) % BW
        if pad:
            cols.append(jnp.zeros((D, pad), w_in.dtype))
            o += pad
    gate_off = o
    a, sz = src['gates']
    w2 = jnp.concatenate(cols + [w_in[:, a:a + sz]], axis=1)
    h2d = hcat.reshape(M, D).astype(bf16)
    px = _mm_w(h2d, w2, 0, gate_off, out_dtype=bf16).reshape(B, Lt, gate_off)

    def part(nm):
        a_, sz_ = dst[nm]
        return px[..., a_:a_ + sz_].astype(f32)

    na_k, na_v, na_q = part('na_k'), part('na_v'), part('na_q')
    rw_k, rw_v, rw_r = part('rw_k'), part('rw_v'), part('rw_r')
    rw_wl, rw_al, rw_gl = part('rw_wl'), part('rw_al'), part('rw_gl')
    ml_k, ml_v, ml_q, ml_o, ml_g = part('ml_k'), part('ml_v'), part('ml_q'), part('ml_o'), part('ml_g')
    fo = part('fo')

    C = BW // FOURIER_GROUPS
    cc, sc = _dft_mats(C, C ** -0.5)
    eye_g = jnp.eye(FOURIER_GROUPS, dtype=f32)
    chan = jnp.concatenate([jnp.kron(eye_g, cc), -jnp.kron(eye_g, sc)], axis=1)
    uc = _mm(fo.reshape(M, BW), chan).reshape(B, Lt, 2 * BW)

    def pos_dft(u, L):
        cl, sl_ = _dft_mats(L, L ** -0.5)
        rhs = jnp.concatenate([jnp.swapaxes(u[..., :BW], 0, 1).reshape(L, B * BW),
                               jnp.swapaxes(u[..., BW:], 0, 1).reshape(L, B * BW)], axis=0)
        y = _mm(jnp.concatenate([cl, sl_], axis=1), rhs)
        return jnp.swapaxes(y.reshape(L, B, BW), 0, 1)

    y_four = jnp.concatenate([pos_dft(uc[:, :Lc], Lc), pos_dft(uc[:, Lc:], S)], axis=1)

    n_rows = S // GRID_W
    win_r = min(NA_WIN_ROWS, n_rows)
    scale = HEAD_DIM ** -0.5
    t = jnp.arange(S)
    q_lat = na_q[:, Lc:].reshape(B, S, H, HEAD_DIM)
    k_lat = na_k[:, Lc:].reshape(B, S, H, HEAD_DIM)
    q_rot = (_axial_rope(q_lat, t // GRID_W, t % GRID_W) * scale).reshape(B, S, BW).astype(bf16)
    k_rot = _axial_rope(k_lat, t // GRID_W, t % GRID_W).reshape(B, S, BW).astype(bf16)
    q_plain = (na_q[:, Lc:] * scale).astype(bf16)
    kc = na_k[:, :Lc].astype(bf16)
    vc = na_v[:, :Lc].astype(bf16)
    bias_tab = _na_bias_table(lp['na_rpb'], win_r)
    y_na_x = _na_attention(q_rot, q_plain, k_rot, na_v[:, Lc:].astype(bf16), kc, vc, bias_tab, n_rows, win_r)
    if last:
        y_na_c = jnp.zeros((B, Lc, BW), f32)
    else:
        y_na_c = _ctx_attention((na_q[:, :Lc] * scale).astype(bf16), kc, vc)
    y_na = jnp.concatenate([y_na_c, y_na_x], axis=1)

    def flip_seg(u):
        return _seg_map(lambda s: jnp.flip(s, axis=1), u, Lc)

    seqs = []
    bonus_in = []
    for d in (0, 1):
        mu = lp['rw_mu_rkv'][d]
        mu_wa = lp['rw_mu_wa'][d]

        def lerp(u, m_):
            return u + m_ * (_seg_map(lambda s: _shift(s, d), u, Lc) - u)

        r_l = lerp(rw_r, mu[:BW])
        k_l = lerp(rw_k, mu[BW:2 * BW])
        v_l = lerp(rw_v, mu[2 * BW:])
        wl_l = lerp(rw_wl[..., d * RW_DECAY_LORA:(d + 1) * RW_DECAY_LORA], mu_wa[:RW_DECAY_LORA])
        al_l = lerp(rw_al[..., d * RW_ICL_LORA:(d + 1) * RW_ICL_LORA], mu_wa[RW_DECAY_LORA:])
        w_pre = lp['rw_w0'][d] + _mm_w(jnp.tanh(wl_l).reshape(M, -1), lp['rw_w_up'][d]).reshape(B, Lt, BW)
        w_log = -jax.nn.softplus(-w_pre) - 0.5
        lw = -jnp.exp(w_log)
        a_icl = jax.nn.sigmoid(lp['rw_a0'][d] + _mm_w(al_l.reshape(M, -1), lp['rw_a_up'][d]).reshape(B, Lt, BW))
        kk = (k_l * lp['rw_k_k']).reshape(B, Lt, H, HEAD_DIM)
        kk = (kk / jnp.maximum(jnp.sqrt(jnp.sum(kk * kk, -1, keepdims=True)), 1e-12)).reshape(B, Lt, BW)
        k_mod = k_l * (1.0 + (a_icl - 1.0) * lp['rw_k_a'])
        seqs.append((lw, k_mod, v_l, -kk, kk * a_icl, r_l))
        bonus_in.append((r_l, k_mod, v_l))
    ys = _rwkv_scan(seqs[0], seqs[1], n_ctx=Lc // RW_CHUNK)
    y_rw = 0.0
    for d in (0, 1):
        y_d = _group_norm(ys[d], lp['rw_gn_g'][d], lp['rw_gn_b'][d], H, RW_GN_EPS)
        r_l, k_mod, v_l = bonus_in[d]
        hs = (B, Lt, H, HEAD_DIM)
        bonus = jnp.sum(r_l.reshape(hs) * k_mod.reshape(hs) * lp['rw_r_k'][d], -1, keepdims=True) * v_l.reshape(hs)
        y_rw = y_rw + y_d + bonus.reshape(B, Lt, BW)
    y_rw = y_rw * _mm_w(jax.nn.sigmoid(rw_gl).reshape(M, -1), lp['rw_g_up']).reshape(B, Lt, BW)

    k_m = jax.nn.silu(_seg_map(lambda s: _conv3(s, lp['ml_conv_k']), ml_k, Lc)) * scale
    q_m = jax.nn.silu(_seg_map(lambda s: _conv3(s, lp['ml_conv_q']), ml_q, Lc))
    g4 = ml_g.reshape(B, Lt, 4, H) + lp['ml_gate_b']
    li = g4[:, :, :2]
    lf = jax.nn.log_sigmoid(g4[:, :, 2:])
    li_s = jnp.concatenate([li[:, :, 0], li[:, :, 1]], axis=0)
    lf_s = jnp.concatenate([lf[:, :, 0], lf[:, :, 1]], axis=0)
    hs_ = _mlstm_scan(q_m, k_m, ml_v, li_s, lf_s, n_ctx=Lc // ML_CHUNK)
    h_sum = hs_[0] + hs_[1]
    y_ml = _group_norm(h_sum, lp['ml_gn_g'], None, H, GN_EPS) * jax.nn.sigmoid(ml_o)

    ys4 = [u.reshape(M, BW).astype(bf16) for u in (y_four, y_na, y_rw, y_ml)]
    acc = _merge(h2d, ys4, w2, gate_off, lp['w_branch'], lp['layer'])
    return _mm_w(acc, lp['w_out'], lead=(lp['layer'],)).reshape(B, Lt, D)


def kernel(x, c, ctx, c_ctx, w_ada, b_ada, g_norm1, g_norm2, w_in, na_rpb, rw_mu_rkv, rw_mu_wa, rw_w_up, rw_w0, rw_a_up, rw_a0, rw_g_up, rw_k_k, rw_k_a, rw_r_k, rw_gn_g, rw_gn_b, ml_conv_q, ml_conv_k, ml_gate_b, ml_gn_g, w_branch, w_out, w_router, b_router, w_gu, b_gu, w_dn, b_dn, g_final):
    B, S, D = x.shape
    Lc = ctx.shape[1]
    depth = w_ada.shape[0]
    mod_in = jnp.concatenate([jax.nn.silu(c), jax.nn.silu(c_ctx)[None]], axis=0)
    mod_in = jnp.pad(mod_in, ((0, (-(B + 1)) % 16), (0, 0)))
    for l in range(depth):
        last = l == depth - 1
        lp = {'w_in': w_in[l], 'na_rpb': na_rpb[l], 'rw_mu_rkv': rw_mu_rkv[l], 'rw_mu_wa': rw_mu_wa[l],
              'rw_w_up': rw_w_up[l], 'rw_w0': rw_w0[l], 'rw_a_up': rw_a_up[l], 'rw_a0': rw_a0[l],
              'rw_g_up': rw_g_up[l], 'rw_k_k': rw_k_k[l], 'rw_k_a': rw_k_a[l], 'rw_r_k': rw_r_k[l],
              'rw_gn_g': rw_gn_g[l], 'rw_gn_b': rw_gn_b[l], 'ml_conv_q': ml_conv_q[l],
              'ml_conv_k': ml_conv_k[l], 'ml_gate_b': ml_gate_b[l], 'ml_gn_g': ml_gn_g[l],
              'w_branch': w_branch, 'w_out': w_out, 'layer': l}
        mod = _mm_w(mod_in, w_ada, lead=(l,)) + b_ada[l]
        mx = jnp.split(mod[:B], 6, axis=-1)
        mc = jnp.split(mod[B], 6, axis=-1)
        hx = _rms_norm(x, g_norm1[l]) * (1.0 + mx[1][:, None]) + mx[0][:, None]
        hc = _rms_norm(ctx, g_norm1[l]) * (1.0 + mc[1]) + mc[0]
        y = _mixer(jnp.concatenate([hc, hx], axis=1), lp, B, Lc, S, last)
        x = x + mx[2][:, None] * y[:, Lc:]
        hx2 = _rms_norm(x, g_norm2[l]) * (1.0 + mx[4][:, None]) + mx[3][:, None]
        if last:
            ym = _moe(hx2.reshape(-1, D), w_router[l], b_router[l], w_gu, b_gu, w_dn, b_dn, l)
            x = x + mx[5][:, None] * ym.reshape(x.shape)
        else:
            ctx = ctx + mc[2] * y[:, :Lc]
            hc2 = _rms_norm(ctx, g_norm2[l]) * (1.0 + mc[4]) + mc[3]
            n_x = B * S
            ym = _moe(jnp.concatenate([hx2.reshape(-1, D), hc2.reshape(-1, D)], axis=0),
                      w_router[l], b_router[l], w_gu, b_gu, w_dn, b_dn, l)
            x = x + mx[5][:, None] * ym[:n_x].reshape(x.shape)
            ctx = ctx + mc[5] * ym[n_x:].reshape(ctx.shape)
    return _rms_norm(x, g_final)
```

```python
import functools
import math

import jax
import jax.numpy as jnp
from jax import lax
from jax.experimental import pallas as pl
from jax.experimental.pallas import tpu as pltpu

f32 = jnp.float32
bf16 = jnp.bfloat16

HEAD_DIM = 64
GRID_W = 64
N_BRANCHES = 4
FOURIER_GROUPS = 4
NA_WIN_ROWS = 8
NA_WIN_COLS = 16
ROPE_THETA = 10000.0
RW_DECAY_LORA = 64
RW_ICL_LORA = 64
RW_GATE_LORA = 128
RW_GN_EPS = 64e-5
ML_CHUNK = 64
RW_CHUNK = 64
TOP_K = 4
SWIGLU_LIMIT = 7.0
SWIGLU_ALPHA = 1.702
MOE_BLOCK = 256
MOE_SPLIT = 4
NORM_EPS = 1e-6
GN_EPS = 1e-5
NEG_BIG = -1e30

LANE = 128
VMEM_LIMIT = 56 * 1024 * 1024


def _bdot(a, b):
    return jnp.dot(a.astype(bf16), b.astype(bf16), preferred_element_type=f32)


def _bdot_nt(a, b):
    return lax.dot_general(a.astype(bf16), b.astype(bf16), (((1,), (1,)), ((), ())),
                           preferred_element_type=f32)


def _bdot_tn(a, b):
    return lax.dot_general(a.astype(bf16), b.astype(bf16), (((0,), (0,)), ((), ())),
                           preferred_element_type=f32)


def _split_dot(tri, x):
    hi = x.astype(bf16)
    lo = (x - hi.astype(f32)).astype(bf16)
    return (jnp.dot(tri, hi, preferred_element_type=f32) + jnp.dot(tri, lo, preferred_element_type=f32))


def _split_dot_r(x, tri):
    hi = x.astype(bf16)
    lo = (x - hi.astype(f32)).astype(bf16)
    return (jnp.dot(hi, tri, preferred_element_type=f32) + jnp.dot(lo, tri, preferred_element_type=f32))


def _pick(dim, cap, align):
    if dim <= cap:
        return dim
    t = cap - cap % align
    while t >= align:
        if dim % t == 0:
            return t
        t -= align
    return dim


def _mm_kernel_single(a_ref, b_ref, o_ref):
    o_ref[...] = jnp.dot(a_ref[...], b_ref[...], preferred_element_type=f32).astype(o_ref.dtype)


def _mm_kernel_acc(a_ref, b_ref, o_ref, acc_ref, *, nk):
    k = pl.program_id(2)

    @pl.when(k == 0)
    def _():
        acc_ref[...] = jnp.zeros_like(acc_ref)

    acc_ref[...] += jnp.dot(a_ref[...], b_ref[...], preferred_element_type=f32)

    @pl.when(k == nk - 1)
    def _():
        o_ref[...] = acc_ref[...].astype(o_ref.dtype)


def _mm(a, b, out_dtype=f32):
    a = a.astype(bf16)
    b = b.astype(bf16)
    M, K = a.shape
    N = b.shape[1]
    tm = _pick(M, 1024, 16)
    tn = _pick(N, 1024, LANE)
    tk = _pick(K, 2048, LANE)
    nk = K // tk
    grid = (M // tm, N // tn, nk)
    in_specs = [pl.BlockSpec((tm, tk), lambda i, j, k: (i, k)),
                pl.BlockSpec((tk, tn), lambda i, j, k: (k, j))]
    out_spec = pl.BlockSpec((tm, tn), lambda i, j, k: (i, j))
    params = pltpu.CompilerParams(dimension_semantics=("parallel", "parallel", "arbitrary"),
                                  vmem_limit_bytes=VMEM_LIMIT)
    if nk == 1:
        return pl.pallas_call(_mm_kernel_single, grid=grid, in_specs=in_specs, out_specs=out_spec,
                              out_shape=jax.ShapeDtypeStruct((M, N), out_dtype),
                              compiler_params=params, name="mm")(a, b)
    return pl.pallas_call(functools.partial(_mm_kernel_acc, nk=nk), grid=grid, in_specs=in_specs,
                          out_specs=out_spec, out_shape=jax.ShapeDtypeStruct((M, N), out_dtype),
                          scratch_shapes=[pltpu.VMEM((tm, tn), f32)],
                          compiler_params=params, name="mm_acc")(a, b)


def _mm_ws_kernel(a_ref, b_ref, o_ref, w_ref):
    @pl.when(pl.program_id(1) == 0)
    def _():
        w_ref[...] = b_ref[...].astype(bf16)

    o_ref[...] = jnp.dot(a_ref[...], w_ref[...], preferred_element_type=f32).astype(o_ref.dtype)


def _mm_w(a, w, col_off=0, n_cols=None, out_dtype=f32, lead=()):
    a = a.astype(bf16)
    M, K = a.shape
    n_cols = w.shape[-1] - col_off if n_cols is None else n_cols
    tm = _pick(M, 1024, 16)
    tn = _pick(n_cols, 512, LANE)
    if col_off % tn:
        tn = math.gcd(tn, col_off)
    off = col_off // tn
    return pl.pallas_call(
        _mm_ws_kernel,
        grid=(n_cols // tn, M // tm),
        in_specs=[pl.BlockSpec((tm, K), lambda j, i: (i, 0)),
                  pl.BlockSpec((None,) * len(lead) + (K, tn), lambda j, i: tuple(lead) + (0, j + off))],
        out_specs=pl.BlockSpec((tm, tn), lambda j, i: (i, j)),
        out_shape=jax.ShapeDtypeStruct((M, n_cols), out_dtype),
        scratch_shapes=[pltpu.VMEM((K, tn), bf16)],
        compiler_params=pltpu.CompilerParams(dimension_semantics=("parallel", "arbitrary"),
                                             vmem_limit_bytes=VMEM_LIMIT),
        name="mm_w")(a, w)


def _merge_kernel(*refs, nb):
    h_ref = refs[0]
    y_refs, g_refs = refs[1:1 + nb], refs[1 + nb:1 + 2 * nb]
    wb_ref, o_ref, wg_scr, wb_scr = refs[1 + 2 * nb:]

    @pl.when(pl.program_id(1) == 0)
    def _():
        for i in range(nb):
            wg_scr[i] = g_refs[i][...].astype(bf16)
        wb_scr[...] = wb_ref[...].astype(bf16)

    h = h_ref[...]
    acc = None
    for i in range(nb):
        gate = jnp.dot(h, wg_scr[i], preferred_element_type=f32)
        p = jnp.dot(y_refs[i][...], wb_scr[i], preferred_element_type=f32)
        t = p / (1.0 + jnp.exp(-gate))
        acc = t if acc is None else acc + t
    o_ref[...] = acc.astype(o_ref.dtype)


def _merge(h, ys, w_all, gate_off, w_branch, layer):
    nb = len(ys)
    M, K = h.shape
    W = ys[0].shape[1]
    D = w_branch.shape[-1]
    tm = _pick(M, 1024, 16)
    tn = math.gcd(_pick(D, 256, LANE), gate_off)
    nj = D // tn
    off = gate_off // tn
    g_specs = [pl.BlockSpec((K, tn), functools.partial(lambda j, i, b: (0, off + b * nj + j), b=b))
               for b in range(nb)]
    return pl.pallas_call(
        functools.partial(_merge_kernel, nb=nb),
        grid=(nj, M // tm),
        in_specs=[pl.BlockSpec((tm, K), lambda j, i: (i, 0))]
                 + [pl.BlockSpec((tm, W), lambda j, i: (i, 0)) for _ in range(nb)]
                 + g_specs + [pl.BlockSpec((None, nb, W, tn), lambda j, i: (layer, 0, 0, j))],
        out_specs=pl.BlockSpec((tm, tn), lambda j, i: (i, j)),
        out_shape=jax.ShapeDtypeStruct((M, D), bf16),
        scratch_shapes=[pltpu.VMEM((nb, K, tn), bf16), pltpu.VMEM((nb, W, tn), bf16)],
        compiler_params=pltpu.CompilerParams(dimension_semantics=("parallel", "arbitrary"),
                                             vmem_limit_bytes=VMEM_LIMIT),
        name="merge")(h, *ys, *([w_all] * nb), w_branch)


def _rwkv_kernel(*refs, n_heads):
    T = RW_CHUNK
    dh = HEAD_DIM
    y_refs, h_ref = refs[12:14], refs[14]

    @pl.when(pl.program_id(1) == 0)
    def _():
        h_ref[...] = jnp.zeros_like(h_ref)

    row = lax.broadcasted_iota(jnp.int32, (T, T), 0)
    col = lax.broadcasted_iota(jnp.int32, (T, T), 1)
    eye = row == col
    zeros = jnp.zeros((T, dh), f32)
    chains = []
    for d in (0, 1):
        strict = row > col if d == 0 else row < col
        incl = strict | eye
        tri = jnp.where(incl, 1.0, 0.0).astype(bf16)
        lw, kk, v_all, aa, bb, rr = [r[0] for r in refs[6 * d:6 * d + 6]]
        c = _split_dot(tri, lw)
        c_end = c[T - 1:T, :] if d == 0 else c[0:1, :]
        e_neg = jnp.exp(-c)
        e_rem = jnp.exp(c_end - c)
        g_end = jnp.exp(c_end)
        at_all = aa * jnp.exp(c - lw)
        rt_all = rr * jnp.exp(c)
        bt_all = bb * e_neg
        kt_all = kk * e_neg
        bp_all = bb * e_rem
        kp_all = kk * e_rem
        for h in range(n_heads):
            s = slice(h * dh, (h + 1) * dh)
            chains.append((at_all[:, s], rt_all[:, s], bt_all[:, s], kt_all[:, s], bp_all[:, s], kp_all[:, s],
                           v_all[:, s], g_end[:, s], strict, incl))
    AT, RT, BT, KT, BP, KP, V, GE, ST, IN = zip(*chains)
    hd = range(len(chains))

    g = [_bdot_nt(jnp.concatenate([AT[i], RT[i]], axis=0), jnp.concatenate([BT[i], KT[i]], axis=0)) for i in hd]
    p = [jnp.where(ST[i], g[i][:T, :T], 0.0) for i in hd]
    u = [_bdot(jnp.where(ST[i], g[i][:T, T:], 0.0), V[i]) for i in hd]
    x = [jnp.concatenate([AT[i], u[i]], axis=1) for i in hd]
    n = 1
    while 2 * n < T:
        px = [_bdot(p[i], jnp.concatenate([x[i], p[i]], axis=1)) for i in hd]
        x = [x[i] + px[i][:, :2 * dh] for i in hd]
        p = [px[i][:, 2 * dh:] for i in hd]
        n *= 2
    x = [x[i] + _bdot(p[i], x[i]) for i in hd]
    z = [jnp.concatenate([x[i], jnp.concatenate([zeros, V[i]], axis=1)], axis=0) for i in hd]
    mx = [_bdot(jnp.concatenate([jnp.where(IN[i], g[i][T:, :T], 0.0), jnp.where(IN[i], g[i][T:, T:], 0.0)], axis=1),
                z[i]) for i in hd]
    bx = [_bdot_tn(jnp.concatenate([BP[i], KP[i]], axis=0), z[i]) for i in hd]
    yh = [_bdot(jnp.concatenate([RT[i] + mx[i][:, :dh], jnp.where(eye, GE[i], 0.0) + bx[i][:, :dh]], axis=0),
                h_ref[i]) for i in hd]
    for i in hd:
        d, h = divmod(i, n_heads)
        y_refs[d][0, :, h * dh:(h + 1) * dh] = yh[i][:T] + mx[i][:, dh:]
        h_ref[i] = yh[i][T:] + bx[i][:, dh:]


def _back_chunk(c, n_ctx, nc):
    return jnp.where(c < n_ctx, n_ctx - 1 - c, nc - 1 - c + n_ctx)


def _rwkv_scan(fwd, bwd, n_ctx):
    B, L, W = fwd[0].shape
    n_heads = W // HEAD_DIM
    nc = L // RW_CHUNK
    blk = (1, RW_CHUNK, W)
    f_spec = pl.BlockSpec(blk, lambda b, c: (b, c, 0))
    b_spec = pl.BlockSpec(blk, lambda b, c: (b, _back_chunk(c, n_ctx, nc), 0))
    return pl.pallas_call(
        functools.partial(_rwkv_kernel, n_heads=n_heads),
        grid=(B, nc),
        in_specs=[f_spec] * 6 + [b_spec] * 6,
        out_specs=[f_spec, b_spec],
        out_shape=[jax.ShapeDtypeStruct((B, L, W), f32)] * 2,
        scratch_shapes=[pltpu.VMEM((2 * n_heads, HEAD_DIM, HEAD_DIM), f32)],
        compiler_params=pltpu.CompilerParams(dimension_semantics=("parallel", "arbitrary"),
                                             vmem_limit_bytes=VMEM_LIMIT),
        name="rwkv")(*fwd, *bwd)


def _mlstm_kernel(*refs, n_heads, dirs):
    T = ML_CHUNK
    dh = HEAD_DIM
    nd_ = len(dirs)
    h_refs, c_ref, m_ref = refs[5 * nd_:6 * nd_], refs[6 * nd_], refs[6 * nd_ + 1]

    @pl.when(pl.program_id(1) == 0)
    def _():
        c_ref[...] = jnp.zeros_like(c_ref)
        m_ref[...] = jnp.zeros_like(m_ref)

    row = lax.broadcasted_iota(jnp.int32, (T, T), 0)
    col = lax.broadcasted_iota(jnp.int32, (T, T), 1)
    one_col = jnp.where(lax.broadcasted_iota(jnp.int32, (T, dh), 1) == 0, 1.0, 0.0)
    chains = []
    for j, d in enumerate(dirs):
        q_ref, k_ref, v_ref, gc_ref, gr_ref = refs[5 * j:5 * j + 5]
        lower = row >= col if d == 0 else row <= col
        tri_l = jnp.where(lower, 1.0, 0.0).astype(bf16)
        tri_u = jnp.where(row <= col if d == 0 else row >= col, 1.0, 0.0).astype(bf16)
        gc = gc_ref[0]
        gr = gr_ref[0, 0]
        li_c = gc[:, :n_heads]
        b_c = _split_dot(tri_l, gc[:, n_heads:])
        li_r = gr[:n_heads, :]
        b_r = _split_dot_r(gr[n_heads:, :], tri_u)
        b_end_all = b_c[T - 1:T, :] if d == 0 else b_c[0:1, :]
        q_all, k_all, v_all = q_ref[0], k_ref[0], v_ref[0]
        for h in range(n_heads):
            s = slice(h * dh, (h + 1) * dh)
            chains.append((q_all[:, s], k_all[:, s], jnp.concatenate([v_all[:, s], one_col], axis=1),
                           b_c[:, h:h + 1], li_c[:, h:h + 1], b_r[h:h + 1, :], li_r[h:h + 1, :],
                           b_end_all[:, h:h + 1], lower))
    Q, K, VA, BC, LIC, BR, LIR, BE, LO = zip(*chains)
    hd = range(len(chains))
    m = [m_ref[i:i + 1, 0:1] for i in hd]
    state = [c_ref[i] for i in hd]

    gcol = [BC[i] + m[i] for i in hd]
    log_d = [jnp.where(LO[i], BC[i] - BR[i] + LIR[i], NEG_BIG) for i in hd]
    m_t = [jnp.maximum(gcol[i], jnp.max(log_d[i], axis=1, keepdims=True)) for i in hd]
    inter = [jnp.exp(gcol[i] - m_t[i]) for i in hd]
    qk = [_bdot_nt(Q[i], K[i]) for i in hd]
    nd = [_bdot(jnp.concatenate([inter[i] * Q[i], qk[i] * jnp.exp(log_d[i] - m_t[i])], axis=1),
                jnp.concatenate([state[i], VA[i]], axis=0)) for i in hd]
    log_in = [BE[i] - BC[i] + LIC[i] for i in hd]
    m_new = [jnp.maximum(BE[i] + m[i], jnp.max(log_in[i], axis=0, keepdims=True)) for i in hd]
    upd = [_bdot_tn(K[i] * jnp.exp(log_in[i] - m_new[i]), VA[i]) for i in hd]
    for i in hd:
        j, h = divmod(i, n_heads)
        h_refs[j][0, :, h * dh:(h + 1) * dh] = (nd[i][:, :dh]
                                                / jnp.maximum(jnp.abs(nd[i][:, dh:dh + 1]), jnp.exp(-m_t[i])))
        c_ref[i] = jnp.exp(BE[i] + m[i] - m_new[i]) * state[i] + upd[i]
        m_ref[i:i + 1, :] = jnp.broadcast_to(m_new[i], (1, LANE))


def _mlstm_scan(q, k, v, li, lf, n_ctx):
    B, L, W = k.shape
    H = W // HEAD_DIM
    nc = L // ML_CHUNK
    gcol = jnp.concatenate([li, lf], axis=-1)
    grow = jnp.swapaxes(gcol.reshape(2 * B, nc, ML_CHUNK, 2 * H), 2, 3)

    def specs(d):
        def chunk(c):
            return c if d == 0 else _back_chunk(c, n_ctx, nc)
        seq = pl.BlockSpec((1, ML_CHUNK, W), lambda b, c: (b, chunk(c), 0))
        return seq, [seq] * 3 + [pl.BlockSpec((1, ML_CHUNK, 2 * H), lambda b, c: (b + d * B, chunk(c), 0)),
                                 pl.BlockSpec((1, 1, 2 * H, ML_CHUNK), lambda b, c: (b + d * B, chunk(c), 0, 0))]

    outs = []
    for d in (0, 1):
        out_spec, in_specs = specs(d)
        outs.append(pl.pallas_call(
            functools.partial(_mlstm_kernel, n_heads=H, dirs=(d,)),
            grid=(B, nc),
            in_specs=in_specs,
            out_specs=[out_spec],
            out_shape=[jax.ShapeDtypeStruct((B, L, W), f32)],
            scratch_shapes=[pltpu.VMEM((H, HEAD_DIM, 2 * HEAD_DIM), f32),
                            pltpu.VMEM((H, LANE), f32)],
            compiler_params=pltpu.CompilerParams(dimension_semantics=("parallel", "arbitrary"),
                                                 vmem_limit_bytes=VMEM_LIMIT),
            name="mlstm")(q, k, v, gcol, grow)[0])
    return outs


def _na_row_start(r, n_rows, win_r):
    return jnp.clip(r - win_r // 2, 0, n_rows - win_r)


def _na_kernel(qr_ref, qp_ref, k_ref, v_ref, kc_ref, vc_ref, bias_ref, o_ref, *, n_heads, n_rows, win_r):
    dh = HEAD_DIM
    r = pl.program_id(1)
    start = pl.multiple_of(_na_row_start(r, n_rows, win_r) * GRID_W, GRID_W)
    kwin = k_ref[0, pl.ds(start, win_r * GRID_W), :]
    vwin = v_ref[0, pl.ds(start, win_r * GRID_W), :]
    qr = qr_ref[0]
    qp = qp_ref[0]
    kc = kc_ref[0]
    vc = vc_ref[0]
    hd = range(n_heads)
    sls = [slice(h * dh, (h + 1) * dh) for h in hd]
    s_loc = [_bdot_nt(qr[:, sls[h]], kwin[:, sls[h]]) + bias_ref[h, 0] for h in hd]
    s_ctx = [_bdot_nt(qp[:, s], kc[:, s]) for s in sls]
    m = [jnp.maximum(jnp.max(s_loc[h], axis=1, keepdims=True), jnp.max(s_ctx[h], axis=1, keepdims=True)) for h in hd]
    p = [jnp.concatenate([jnp.exp(s_loc[h] - m[h]), jnp.exp(s_ctx[h] - m[h])], axis=1) for h in hd]
    o = [_bdot(p[h], jnp.concatenate([vwin[:, sls[h]], vc[:, sls[h]]], axis=0)) for h in hd]
    o_ref[0] = jnp.concatenate([o[h] / jnp.sum(p[h], axis=1, keepdims=True) for h in hd],
                               axis=1).astype(o_ref.dtype)


def _na_attention(q_rot, q_plain, k_rot, v, kc, vc, bias_tab, n_rows, win_r):
    B, S, W = q_rot.shape
    Lc = kc.shape[1]
    H = W // HEAD_DIM
    n_var = bias_tab.shape[1]
    qspec = pl.BlockSpec((1, GRID_W, W), lambda b, r: (b, r, 0))
    full = pl.BlockSpec((1, S, W), lambda b, r: (b, 0, 0))
    cspec = pl.BlockSpec((1, Lc, W), lambda b, r: (b, 0, 0))

    def bias_map(b, r):
        return (0, _na_row_start(r, n_rows, win_r) - r + win_r - 1, 0, 0)

    return pl.pallas_call(
        functools.partial(_na_kernel, n_heads=H, n_rows=n_rows, win_r=win_r),
        grid=(B, n_rows),
        in_specs=[qspec, qspec, full, full, cspec, cspec,
                  pl.BlockSpec((H, 1, GRID_W, win_r * GRID_W), bias_map)],
        out_specs=qspec,
        out_shape=jax.ShapeDtypeStruct((B, S, W), bf16),
        compiler_params=pltpu.CompilerParams(dimension_semantics=("parallel", "arbitrary"),
                                             vmem_limit_bytes=VMEM_LIMIT),
        name="na")(q_rot, q_plain, k_rot, v, kc, vc, bias_tab)


def _ctx_attn_kernel(q_ref, k_ref, v_ref, o_ref, *, n_heads):
    dh = HEAD_DIM
    q = q_ref[0]
    k = k_ref[0]
    v = v_ref[0]
    for h in range(n_heads):
        sl = slice(h * dh, (h + 1) * dh)
        s = _bdot_nt(q[:, sl], k[:, sl])
        m = jnp.max(s, axis=1, keepdims=True)
        p = jnp.exp(s - m)
        o_ref[0, :, sl] = _bdot(p, v[:, sl]) / jnp.sum(p, axis=1, keepdims=True)


def _ctx_attention(q, k, v):
    B, Lc, W = q.shape
    spec = pl.BlockSpec((1, Lc, W), lambda b: (b, 0, 0))
    return pl.pallas_call(
        functools.partial(_ctx_attn_kernel, n_heads=W // HEAD_DIM),
        grid=(B,), in_specs=[spec] * 3, out_specs=spec,
        out_shape=jax.ShapeDtypeStruct((B, Lc, W), f32),
        compiler_params=pltpu.CompilerParams(dimension_semantics=("parallel",), vmem_limit_bytes=VMEM_LIMIT),
        name="ctx_attn")(q, k, v)


def _moe_kernel(be_ref, nused_ref, x_ref, wgu_ref, bgu_ref, wdn_ref, bdn_ref, prev_ref, o_ref,
                wgu_scr, wdn_scr, *, d_expert):
    del prev_ref
    i = pl.program_id(0)

    @pl.when((i == 0) | (be_ref[i] != be_ref[jnp.maximum(i - 1, 0)]))
    def _():
        wgu_scr[...] = wgu_ref[0].astype(bf16)
        wdn_scr[...] = wdn_ref[0].astype(bf16)

    @pl.when(i < nused_ref[0])
    def _():
        gu = jnp.dot(x_ref[...], wgu_scr[...], preferred_element_type=f32) + bgu_ref[0]
        gt = jnp.minimum(gu[:, :d_expert], SWIGLU_LIMIT)
        up = jnp.clip(gu[:, d_expert:], -SWIGLU_LIMIT, SWIGLU_LIMIT)
        act = (up + 1.0) * (gt / (1.0 + jnp.exp(-SWIGLU_ALPHA * gt)))
        y = jnp.dot(act.astype(bf16), wdn_scr[...], preferred_element_type=f32) + bdn_ref[0]
        o_ref[...] = y.astype(o_ref.dtype)

    @pl.when(i >= nused_ref[0])
    def _():
        o_ref[...] = jnp.zeros_like(o_ref)


def _moe_experts(xg, block_e, n_used, w_gu, b_gu, w_dn, b_dn, layer, ys, blk_off):
    R, D = xg.shape
    n_blocks = R // MOE_BLOCK
    F2 = w_gu.shape[-1]
    grid_spec = pltpu.PrefetchScalarGridSpec(
        num_scalar_prefetch=2,
        grid=(n_blocks,),
        in_specs=[pl.BlockSpec((MOE_BLOCK, D), lambda i, be, nu: (i, 0)),
                  pl.BlockSpec((None, 1, D, F2), lambda i, be, nu: (layer, be[i], 0, 0)),
                  pl.BlockSpec((None, 1, 1, F2), lambda i, be, nu: (layer, be[i], 0, 0)),
                  pl.BlockSpec((None, 1, F2 // 2, D), lambda i, be, nu: (layer, be[i], 0, 0)),
                  pl.BlockSpec((None, 1, 1, D), lambda i, be, nu: (layer, be[i], 0, 0)),
                  pl.BlockSpec(memory_space=pl.ANY)],
        out_specs=pl.BlockSpec((MOE_BLOCK, D), lambda i, be, nu: (i + blk_off, 0)),
        scratch_shapes=[pltpu.VMEM((D, F2), bf16), pltpu.VMEM((F2 // 2, D), bf16)])
    return pl.pallas_call(
        functools.partial(_moe_kernel, d_expert=F2 // 2),
        grid_spec=grid_spec,
        out_shape=jax.ShapeDtypeStruct(ys.shape, ys.dtype),
        input_output_aliases={7: 0},
        compiler_params=pltpu.CompilerParams(dimension_semantics=("arbitrary",), vmem_limit_bytes=VMEM_LIMIT),
        name="moe")(block_e, n_used, xg, w_gu, b_gu[:, :, None, :], w_dn, b_dn[:, :, None, :], ys)


def _moe(h, w_router, b_router, w_gu, b_gu, w_dn, b_dn, layer):
    T, D = h.shape
    E = w_router.shape[1]
    A = T * TOP_K
    h_hi = h.astype(bf16)
    h_lo = (h - h_hi.astype(f32)).astype(bf16)
    w_hi = w_router.astype(bf16)
    w_lo = (w_router - w_hi.astype(f32)).astype(bf16)
    logits = _mm(h_hi, w_hi) + _mm(h_lo, w_hi) + _mm(h_hi, w_lo) + b_router
    top_val, top_idx = lax.top_k(logits, TOP_K)
    gate = jax.nn.softmax(top_val, axis=-1)
    flat_e = top_idx.reshape(-1).astype(jnp.int32)
    onehot = flat_e[:, None] == jnp.arange(E, dtype=jnp.int32)
    counts = jnp.sum(onehot.astype(jnp.int32), axis=0)
    padded = (counts + MOE_BLOCK - 1) // MOE_BLOCK * MOE_BLOCK
    pad_end = jnp.cumsum(padded)
    pad_start = pad_end - padded
    start = jnp.cumsum(counts) - counts
    order = jnp.argsort(flat_e).astype(jnp.int32)
    inv = jnp.argsort(order).astype(jnp.int32)
    shift_e = pad_start - start
    pos = (inv + jnp.sum(jnp.where(onehot, shift_e, 0), axis=1)).reshape(T, TOP_K)
    n_blocks = (A + MOE_BLOCK - 1) // MOE_BLOCK + E
    blk_first = jnp.arange(n_blocks, dtype=jnp.int32) * MOE_BLOCK
    block_e = jnp.minimum(jnp.sum((pad_end[None, :] <= blk_first[:, None]).astype(jnp.int32), axis=1), E - 1)
    n_used = (pad_end[-1:] // MOE_BLOCK).astype(jnp.int32)
    slot = jnp.arange(n_blocks * MOE_BLOCK, dtype=jnp.int32).reshape(n_blocks, MOE_BLOCK)
    off_in = slot - pad_start[block_e][:, None]
    valid = off_in < counts[block_e][:, None]
    srt = jnp.clip(start[block_e][:, None] + off_in, 0, A - 1)
    tok_buf = jnp.where(valid, order[srt.reshape(-1)].reshape(srt.shape) // TOP_K, 0).reshape(-1)
    ys = jnp.zeros((n_blocks * MOE_BLOCK, D), bf16)
    per = -(-n_blocks // MOE_SPLIT)
    for q in range(MOE_SPLIT):
        lo, hi = q * per, min((q + 1) * per, n_blocks)
        if lo >= hi:
            break
        xg = h_hi[tok_buf[lo * MOE_BLOCK:hi * MOE_BLOCK]]
        ys = _moe_experts(xg, block_e[lo:hi], n_used - lo, w_gu, b_gu, w_dn, b_dn, layer, ys, lo)
    out = gate[:, 0:1] * ys[pos[:, 0]]
    for j in range(1, TOP_K):
        out = out + gate[:, j:j + 1] * ys[pos[:, j]]
    return out


def _rms_norm(x, g):
    return x * lax.rsqrt(jnp.mean(x * x, -1, keepdims=True) + NORM_EPS) * g


def _group_norm(y, g, b, n_heads, eps):
    shp = y.shape
    yh = y.reshape(shp[:-1] + (n_heads, shp[-1] // n_heads))
    mu = jnp.mean(yh, -1, keepdims=True)
    var = jnp.mean(jnp.square(yh - mu), -1, keepdims=True)
    out = ((yh - mu) * lax.rsqrt(var + eps)).reshape(shp) * g
    return out if b is None else out + b


def _dft_mats(n, scale):
    j = jnp.arange(n, dtype=jnp.int32)
    m = (j[:, None] * j[None, :]) % n
    ang = m.astype(f32) * (2.0 * math.pi / n)
    return jnp.cos(ang) * scale, jnp.sin(ang) * scale


def _axial_rope(t, rows, cols):
    half = HEAD_DIM // 2
    nf = half // 2
    inv = ROPE_THETA ** (-jnp.arange(nf, dtype=f32) / nf)

    def rot(u, pos):
        ang = pos.astype(f32)[:, None] * inv
        cos = jnp.cos(ang)[None, :, None, :]
        sin = jnp.sin(ang)[None, :, None, :]
        u1, u2 = u[..., :nf], u[..., nf:]
        return jnp.concatenate([u1 * cos - u2 * sin, u1 * sin + u2 * cos], -1)

    return jnp.concatenate([rot(t[..., :half], rows), rot(t[..., half:], cols)], -1)


def _seg_map(fn, u, Lc):
    return jnp.concatenate([fn(u[:, :Lc]), fn(u[:, Lc:])], axis=1)


def _shift(u, direction):
    if direction == 0:
        return jnp.pad(u[:, :-1], ((0, 0), (1, 0), (0, 0)))
    return jnp.pad(u[:, 1:], ((0, 0), (0, 1), (0, 0)))


def _conv3(u, w):
    return w[0] * _shift(u, 0) + w[1] * u + w[2] * _shift(u, 1)


def _na_bias_table(rpb, win_r):
    H = rpb.shape[0]
    w = jnp.arange(GRID_W)
    col_start = jnp.clip(w - NA_WIN_COLS // 2, 0, GRID_W - NA_WIN_COLS)
    col_in = (w[None, :] >= col_start[:, None]) & (w[None, :] < col_start[:, None] + NA_WIN_COLS)
    col_off = jnp.clip(w[None, :] - w[:, None], 1 - NA_WIN_COLS, NA_WIN_COLS - 1) + NA_WIN_COLS - 1
    per_row = jnp.where(col_in[None, None], rpb[:, :, col_off], NEG_BIG)
    tabs = []
    for j in range(win_r):
        d0 = j - (win_r - 1)
        rows = [per_row[:, d0 + a + NA_WIN_ROWS - 1] for a in range(win_r)]
        tabs.append(jnp.concatenate(rows, axis=-1))
    return jnp.stack(tabs, axis=1)


def _mixer(hcat, lp, B, Lc, S, last):
    D = hcat.shape[-1]
    BW = D // N_BRANCHES
    H = BW // HEAD_DIM
    Lt = Lc + S
    M = B * Lt
    w_in = lp['w_in']
    names = ('na_k', 'na_v', 'rw_k', 'rw_v', 'rw_wl', 'rw_al', 'ml_k', 'ml_v', 'ml_g',
             'fo', 'na_q', 'rw_r', 'rw_gl', 'ml_q', 'ml_o', 'gates')
    sizes = (BW, BW, BW, BW, 2 * RW_DECAY_LORA, 2 * RW_ICL_LORA, BW, BW, 4 * H,
             BW, BW, BW, RW_GATE_LORA, BW, BW, N_BRANCHES * D)
    src, o = {}, 0
    for nm, sz in zip(names, sizes):
        src[nm] = (o, sz)
        o += sz
    groups = (('na_k', 'na_v', 'na_q'), ('rw_k', 'rw_v', 'rw_r', 'rw_wl', 'rw_al', 'rw_gl'),
              ('ml_k', 'ml_v', 'ml_q', 'ml_o', 'ml_g'), ('fo',))
    cols, dst, o = [], {}, 0
    for grp in groups:
        for nm in grp:
            a, sz = src[nm]
            cols.append(w_in[:, a:a + sz])
            dst[nm] = (o, sz)
            o += sz
        pad = (-o) % BW
        if pad:
            cols.append(jnp.zeros((D, pad), w_in.dtype))
            o += pad
    gate_off = o
    a, sz = src['gates']
    w2 = jnp.concatenate(cols + [w_in[:, a:a + sz]], axis=1)
    h2d = hcat.reshape(M, D).astype(bf16)
    px = _mm_w(h2d, w2, 0, gate_off, out_dtype=bf16).reshape(B, Lt, gate_off)

    def part(nm):
        a_, sz_ = dst[nm]
        return px[..., a_:a_ + sz_].astype(f32)

    na_k, na_v, na_q = part('na_k'), part('na_v'), part('na_q')
    rw_k, rw_v, rw_r = part('rw_k'), part('rw_v'), part('rw_r')
    rw_wl, rw_al, rw_gl = part('rw_wl'), part('rw_al'), part('rw_gl')
    ml_k, ml_v, ml_q, ml_o, ml_g = part('ml_k'), part('ml_v'), part('ml_q'), part('ml_o'), part('ml_g')
    fo = part('fo')

    C = BW // FOURIER_GROUPS
    cc, sc = _dft_mats(C, C ** -0.5)
    eye_g = jnp.eye(FOURIER_GROUPS, dtype=f32)
    chan = jnp.concatenate([jnp.kron(eye_g, cc), -jnp.kron(eye_g, sc)], axis=1)
    uc = _mm(fo.reshape(M, BW), chan).reshape(B, Lt, 2 * BW)

    def pos_dft(u, L):
        cl, sl_ = _dft_mats(L, L ** -0.5)
        rhs = jnp.concatenate([jnp.swapaxes(u[..., :BW], 0, 1).reshape(L, B * BW),
                               jnp.swapaxes(u[..., BW:], 0, 1).reshape(L, B * BW)], axis=0)
        y = _mm(jnp.concatenate([cl, sl_], axis=1), rhs, out_dtype=bf16)
        return jnp.swapaxes(y.reshape(L, B, BW), 0, 1)

    y_four = jnp.concatenate([pos_dft(uc[:, :Lc], Lc), pos_dft(uc[:, Lc:], S)], axis=1)

    n_rows = S // GRID_W
    win_r = min(NA_WIN_ROWS, n_rows)
    scale = HEAD_DIM ** -0.5
    t = jnp.arange(S)
    q_lat = na_q[:, Lc:].reshape(B, S, H, HEAD_DIM)
    k_lat = na_k[:, Lc:].reshape(B, S, H, HEAD_DIM)
    q_rot = (_axial_rope(q_lat, t // GRID_W, t % GRID_W) * scale).reshape(B, S, BW).astype(bf16)
    k_rot = _axial_rope(k_lat, t // GRID_W, t % GRID_W).reshape(B, S, BW).astype(bf16)
    q_plain = (na_q[:, Lc:] * scale).astype(bf16)
    kc = na_k[:, :Lc].astype(bf16)
    vc = na_v[:, :Lc].astype(bf16)
    bias_tab = _na_bias_table(lp['na_rpb'], win_r)
    y_na_x = _na_attention(q_rot, q_plain, k_rot, na_v[:, Lc:].astype(bf16), kc, vc, bias_tab, n_rows, win_r)
    if last:
        y_na_c = jnp.zeros((B, Lc, BW), bf16)
    else:
        y_na_c = _ctx_attention((na_q[:, :Lc] * scale).astype(bf16), kc, vc).astype(bf16)
    y_na = jnp.concatenate([y_na_c, y_na_x], axis=1)

    def flip_seg(u):
        return _seg_map(lambda s: jnp.flip(s, axis=1), u, Lc)

    seqs = []
    bonus_in = []
    for d in (0, 1):
        mu = lp['rw_mu_rkv'][d]
        mu_wa = lp['rw_mu_wa'][d]

        def lerp(u, m_):
            return u + m_ * (_seg_map(lambda s: _shift(s, d), u, Lc) - u)

        r_l = lerp(rw_r, mu[:BW])
        k_l = lerp(rw_k, mu[BW:2 * BW])
        v_l = lerp(rw_v, mu[2 * BW:])
        wl_l = lerp(rw_wl[..., d * RW_DECAY_LORA:(d + 1) * RW_DECAY_LORA], mu_wa[:RW_DECAY_LORA])
        al_l = lerp(rw_al[..., d * RW_ICL_LORA:(d + 1) * RW_ICL_LORA], mu_wa[RW_DECAY_LORA:])
        w_pre = lp['rw_w0'][d] + _mm_w(jnp.tanh(wl_l).reshape(M, -1), lp['rw_w_up'][d]).reshape(B, Lt, BW)
        w_log = -jax.nn.softplus(-w_pre) - 0.5
        lw = -jnp.exp(w_log)
        a_icl = jax.nn.sigmoid(lp['rw_a0'][d] + _mm_w(al_l.reshape(M, -1), lp['rw_a_up'][d]).reshape(B, Lt, BW))
        kk = (k_l * lp['rw_k_k']).reshape(B, Lt, H, HEAD_DIM)
        kk = (kk / jnp.maximum(jnp.sqrt(jnp.sum(kk * kk, -1, keepdims=True)), 1e-12)).reshape(B, Lt, BW)
        k_mod = k_l * (1.0 + (a_icl - 1.0) * lp['rw_k_a'])
        seqs.append((lw, k_mod, v_l, -kk, kk * a_icl, r_l))
        bonus_in.append((r_l, k_mod, v_l))
    ys = _rwkv_scan(seqs[0], seqs[1], n_ctx=Lc // RW_CHUNK)
    y_rw = 0.0
    for d in (0, 1):
        y_d = _group_norm(ys[d], lp['rw_gn_g'][d], lp['rw_gn_b'][d], H, RW_GN_EPS)
        r_l, k_mod, v_l = bonus_in[d]
        hs = (B, Lt, H, HEAD_DIM)
        bonus = jnp.sum(r_l.reshape(hs) * k_mod.reshape(hs) * lp['rw_r_k'][d], -1, keepdims=True) * v_l.reshape(hs)
        y_rw = y_rw + y_d + bonus.reshape(B, Lt, BW)
    y_rw = y_rw * _mm_w(jax.nn.sigmoid(rw_gl).reshape(M, -1), lp['rw_g_up']).reshape(B, Lt, BW)

    k_m = jax.nn.silu(_seg_map(lambda s: _conv3(s, lp['ml_conv_k']), ml_k, Lc)) * scale
    q_m = jax.nn.silu(_seg_map(lambda s: _conv3(s, lp['ml_conv_q']), ml_q, Lc))
    g4 = ml_g.reshape(B, Lt, 4, H) + lp['ml_gate_b']
    li = g4[:, :, :2]
    lf = jax.nn.log_sigmoid(g4[:, :, 2:])
    li_s = jnp.concatenate([li[:, :, 0], li[:, :, 1]], axis=0)
    lf_s = jnp.concatenate([lf[:, :, 0], lf[:, :, 1]], axis=0)
    hs_ = _mlstm_scan(q_m, k_m, ml_v, li_s, lf_s, n_ctx=Lc // ML_CHUNK)
    h_sum = hs_[0] + hs_[1]
    y_ml = _group_norm(h_sum, lp['ml_gn_g'], None, H, GN_EPS) * jax.nn.sigmoid(ml_o)

    ys4 = [u.reshape(M, BW).astype(bf16) for u in (y_four, y_na, y_rw, y_ml)]
    acc = _merge(h2d, ys4, w2, gate_off, lp['w_branch'], lp['layer'])
    return _mm_w(acc, lp['w_out'], lead=(lp['layer'],)).reshape(B, Lt, D)


def kernel(x, c, ctx, c_ctx, w_ada, b_ada, g_norm1, g_norm2, w_in, na_rpb, rw_mu_rkv, rw_mu_wa, rw_w_up, rw_w0, rw_a_up, rw_a0, rw_g_up, rw_k_k, rw_k_a, rw_r_k, rw_gn_g, rw_gn_b, ml_conv_q, ml_conv_k, ml_gate_b, ml_gn_g, w_branch, w_out, w_router, b_router, w_gu, b_gu, w_dn, b_dn, g_final):
    B, S, D = x.shape
    Lc = ctx.shape[1]
    depth = w_ada.shape[0]
    mod_in = jnp.concatenate([jax.nn.silu(c), jax.nn.silu(c_ctx)[None]], axis=0)
    mod_in = jnp.pad(mod_in, ((0, (-(B + 1)) % 16), (0, 0)))
    for l in range(depth):
        last = l == depth - 1
        lp = {'w_in': w_in[l], 'na_rpb': na_rpb[l], 'rw_mu_rkv': rw_mu_rkv[l], 'rw_mu_wa': rw_mu_wa[l],
              'rw_w_up': rw_w_up[l], 'rw_w0': rw_w0[l], 'rw_a_up': rw_a_up[l], 'rw_a0': rw_a0[l],
              'rw_g_up': rw_g_up[l], 'rw_k_k': rw_k_k[l], 'rw_k_a': rw_k_a[l], 'rw_r_k': rw_r_k[l],
              'rw_gn_g': rw_gn_g[l], 'rw_gn_b': rw_gn_b[l], 'ml_conv_q': ml_conv_q[l],
              'ml_conv_k': ml_conv_k[l], 'ml_gate_b': ml_gate_b[l], 'ml_gn_g': ml_gn_g[l],
              'w_branch': w_branch, 'w_out': w_out, 'layer': l}
        mod = _mm_w(mod_in, w_ada, lead=(l,)) + b_ada[l]
        mx = jnp.split(mod[:B], 6, axis=-1)
        mc = jnp.split(mod[B], 6, axis=-1)
        hx = _rms_norm(x, g_norm1[l]) * (1.0 + mx[1][:, None]) + mx[0][:, None]
        hc = _rms_norm(ctx, g_norm1[l]) * (1.0 + mc[1]) + mc[0]
        y = _mixer(jnp.concatenate([hc, hx], axis=1), lp, B, Lc, S, last)
        x = x + mx[2][:, None] * y[:, Lc:]
        hx2 = _rms_norm(x, g_norm2[l]) * (1.0 + mx[4][:, None]) + mx[3][:, None]
        if last:
            ym = _moe(hx2.reshape(-1, D), w_router[l], b_router[l], w_gu, b_gu, w_dn, b_dn, l)
            x = x + mx[5][:, None] * ym.reshape(x.shape)
        else:
            ctx = ctx + mc[2] * y[:, :Lc]
            hc2 = _rms_norm(ctx, g_norm2[l]) * (1.0 + mc[4]) + mc[3]
            n_x = B * S
            ym = _moe(jnp.concatenate([hx2.reshape(-1, D), hc2.reshape(-1, D)], axis=0),
                      w_router[l], b_router[l], w_gu, b_gu, w_dn, b_dn, l)
            x = x + mx[5][:, None] * ym[:n_x].reshape(x.shape)
            ctx = ctx + mc[5] * ym[n_x:].reshape(ctx.shape)
    return _rms_norm(x, g_final)
```

```python
import functools
import math

import jax
import jax.numpy as jnp
from jax import lax
from jax.experimental import pallas as pl
from jax.experimental.pallas import tpu as pltpu

f32 = jnp.float32
bf16 = jnp.bfloat16

HEAD_DIM = 64
GRID_W = 64
N_BRANCHES = 4
FOURIER_GROUPS = 4
NA_WIN_ROWS = 8
NA_WIN_COLS = 16
ROPE_THETA = 10000.0
RW_DECAY_LORA = 64
RW_ICL_LORA = 64
RW_GATE_LORA = 128
RW_GN_EPS = 64e-5
ML_CHUNK = 64
RW_CHUNK = 64
TOP_K = 4
SWIGLU_LIMIT = 7.0
SWIGLU_ALPHA = 1.702
MOE_BLOCK = 512
MOE_SPLIT = 4
NORM_EPS = 1e-6
GN_EPS = 1e-5
NEG_BIG = -1e30

LANE = 128
VMEM_LIMIT = 56 * 1024 * 1024


def _bdot(a, b):
    return jnp.dot(a.astype(bf16), b.astype(bf16), preferred_element_type=f32)


def _bdot_nt(a, b):
    return lax.dot_general(a.astype(bf16), b.astype(bf16), (((1,), (1,)), ((), ())),
                           preferred_element_type=f32)


def _bdot_tn(a, b):
    return lax.dot_general(a.astype(bf16), b.astype(bf16), (((0,), (0,)), ((), ())),
                           preferred_element_type=f32)


def _split_dot(tri, x):
    hi = x.astype(bf16)
    lo = (x - hi.astype(f32)).astype(bf16)
    return (jnp.dot(tri, hi, preferred_element_type=f32) + jnp.dot(tri, lo, preferred_element_type=f32))


def _split_dot_r(x, tri):
    hi = x.astype(bf16)
    lo = (x - hi.astype(f32)).astype(bf16)
    return (jnp.dot(hi, tri, preferred_element_type=f32) + jnp.dot(lo, tri, preferred_element_type=f32))


def _running_max(x, d):
    T = x.shape[0]
    t = lax.broadcasted_iota(jnp.int32, x.shape, 0)
    s = 1
    while s < T:
        if d == 0:
            x = jnp.maximum(x, jnp.where(t >= s, pltpu.roll(x, s, axis=0), NEG_BIG))
        else:
            x = jnp.maximum(x, jnp.where(t < T - s, pltpu.roll(x, T - s, axis=0), NEG_BIG))
        s *= 2
    return x


def _pick(dim, cap, align):
    if dim <= cap:
        return dim
    t = cap - cap % align
    while t >= align:
        if dim % t == 0:
            return t
        t -= align
    return dim


def _mm_kernel_single(a_ref, b_ref, o_ref):
    o_ref[...] = jnp.dot(a_ref[...], b_ref[...], preferred_element_type=f32).astype(o_ref.dtype)


def _mm_kernel_acc(a_ref, b_ref, o_ref, acc_ref, *, nk):
    k = pl.program_id(2)

    @pl.when(k == 0)
    def _():
        acc_ref[...] = jnp.zeros_like(acc_ref)

    acc_ref[...] += jnp.dot(a_ref[...], b_ref[...], preferred_element_type=f32)

    @pl.when(k == nk - 1)
    def _():
        o_ref[...] = acc_ref[...].astype(o_ref.dtype)


def _mm(a, b, out_dtype=f32):
    a = a.astype(bf16)
    b = b.astype(bf16)
    M, K = a.shape
    N = b.shape[1]
    tm = _pick(M, 1024, 16)
    tn = _pick(N, 1024, LANE)
    tk = _pick(K, 2048, LANE)
    nk = K // tk
    grid = (M // tm, N // tn, nk)
    in_specs = [pl.BlockSpec((tm, tk), lambda i, j, k: (i, k)),
                pl.BlockSpec((tk, tn), lambda i, j, k: (k, j))]
    out_spec = pl.BlockSpec((tm, tn), lambda i, j, k: (i, j))
    params = pltpu.CompilerParams(dimension_semantics=("parallel", "parallel", "arbitrary"),
                                  vmem_limit_bytes=VMEM_LIMIT)
    if nk == 1:
        return pl.pallas_call(_mm_kernel_single, grid=grid, in_specs=in_specs, out_specs=out_spec,
                              out_shape=jax.ShapeDtypeStruct((M, N), out_dtype),
                              compiler_params=params, name="mm")(a, b)
    return pl.pallas_call(functools.partial(_mm_kernel_acc, nk=nk), grid=grid, in_specs=in_specs,
                          out_specs=out_spec, out_shape=jax.ShapeDtypeStruct((M, N), out_dtype),
                          scratch_shapes=[pltpu.VMEM((tm, tn), f32)],
                          compiler_params=params, name="mm_acc")(a, b)


def _mm_ws_kernel(a_ref, b_ref, o_ref, w_ref):
    @pl.when(pl.program_id(1) == 0)
    def _():
        w_ref[...] = b_ref[...].astype(bf16)

    o_ref[...] = jnp.dot(a_ref[...], w_ref[...], preferred_element_type=f32).astype(o_ref.dtype)


def _mm_w(a, w, col_off=0, n_cols=None, out_dtype=f32, lead=()):
    a = a.astype(bf16)
    M, K = a.shape
    n_cols = w.shape[-1] - col_off if n_cols is None else n_cols
    tm = _pick(M, 1024, 16)
    tn = _pick(n_cols, 512, LANE)
    if col_off % tn:
        tn = math.gcd(tn, col_off)
    off = col_off // tn
    return pl.pallas_call(
        _mm_ws_kernel,
        grid=(n_cols // tn, M // tm),
        in_specs=[pl.BlockSpec((tm, K), lambda j, i: (i, 0)),
                  pl.BlockSpec((None,) * len(lead) + (K, tn), lambda j, i: tuple(lead) + (0, j + off))],
        out_specs=pl.BlockSpec((tm, tn), lambda j, i: (i, j)),
        out_shape=jax.ShapeDtypeStruct((M, n_cols), out_dtype),
        scratch_shapes=[pltpu.VMEM((K, tn), bf16)],
        compiler_params=pltpu.CompilerParams(dimension_semantics=("parallel", "arbitrary"),
                                             vmem_limit_bytes=VMEM_LIMIT),
        name="mm_w")(a, w)


def _norm_kernel(*refs, has_res, emit_x, split):
    x = refs[0][0]
    n_in = 1
    if has_res:
        x = x + refs[2][0] * refs[1][...]
        n_in = 3
    g, scale, shift = refs[n_in][...], refs[n_in + 1][0], refs[n_in + 2][0]
    outs = refs[n_in + 3:]
    h = x * lax.rsqrt(jnp.mean(x * x, axis=-1, keepdims=True) + NORM_EPS) * g * (1.0 + scale) + shift
    if emit_x:
        outs[0][0] = x
        outs = outs[1:]
    if split:
        hi = h.astype(bf16)
        outs[0][0] = hi
        outs[1][0] = (h - hi.astype(f32)).astype(bf16)
    else:
        outs[0][0] = h


def _norm_mod(x, g, scale, shift, res=None, emit_x=True, split=True):
    B, S, D = x.shape
    tm = math.gcd(S, 256)
    row = pl.BlockSpec((1, tm, D), lambda b, i: (b, i, 0))
    vec = pl.BlockSpec((1, 1, D), lambda b, i: (b, 0, 0))
    args, in_specs = [x], [row]
    if res is not None:
        y2d, row0, row_stride, gate = res
        tm = math.gcd(math.gcd(tm, row_stride), row0) if row0 else math.gcd(tm, row_stride)
        row = pl.BlockSpec((1, tm, D), lambda b, i: (b, i, 0))
        args += [y2d, gate[:, None, :]]
        in_specs = [row, pl.BlockSpec((tm, D), lambda b, i: ((row0 + b * row_stride) // tm + i, 0)), vec]
    args += [g[None, :], scale[:, None, :], shift[:, None, :]]
    in_specs += [pl.BlockSpec((1, D), lambda b, i: (0, 0)), vec, vec]
    emit_x = emit_x and res is not None
    out_shape, out_specs = [], []
    if emit_x:
        out_shape.append(jax.ShapeDtypeStruct((B, S, D), f32))
        out_specs.append(row)
    for dt in ((bf16, bf16) if split else (f32,)):
        out_shape.append(jax.ShapeDtypeStruct((B, S, D), dt))
        out_specs.append(row)
    return pl.pallas_call(
        functools.partial(_norm_kernel, has_res=res is not None, emit_x=emit_x, split=split),
        grid=(B, S // tm), in_specs=in_specs, out_specs=out_specs, out_shape=out_shape,
        compiler_params=pltpu.CompilerParams(dimension_semantics=("parallel", "parallel"),
                                             vmem_limit_bytes=VMEM_LIMIT),
        name="norm_mod")(*args)


def _merge_kernel(*refs, nb):
    h_ref = refs[0]
    y_refs, g_refs = refs[1:1 + nb], refs[1 + nb:1 + 2 * nb]
    wb_ref, o_ref, wg_scr, wb_scr = refs[1 + 2 * nb:]

    @pl.when(pl.program_id(1) == 0)
    def _():
        for i in range(nb):
            wg_scr[i] = g_refs[i][...].astype(bf16)
        wb_scr[...] = wb_ref[...].astype(bf16)

    h = h_ref[...]
    acc = None
    for i in range(nb):
        gate = jnp.dot(h, wg_scr[i], preferred_element_type=f32)
        p = jnp.dot(y_refs[i][...], wb_scr[i], preferred_element_type=f32)
        t = p / (1.0 + jnp.exp(-gate))
        acc = t if acc is None else acc + t
    o_ref[...] = acc.astype(o_ref.dtype)


def _merge(h, ys, w_all, gate_off, w_branch, layer):
    nb = len(ys)
    M, K = h.shape
    W = ys[0].shape[1]
    D = w_branch.shape[-1]
    tm = _pick(M, 1024, 16)
    tn = math.gcd(_pick(D, 256, LANE), gate_off)
    nj = D // tn
    off = gate_off // tn
    g_specs = [pl.BlockSpec((K, tn), functools.partial(lambda j, i, b: (0, off + b * nj + j), b=b))
               for b in range(nb)]
    return pl.pallas_call(
        functools.partial(_merge_kernel, nb=nb),
        grid=(nj, M // tm),
        in_specs=[pl.BlockSpec((tm, K), lambda j, i: (i, 0))]
                 + [pl.BlockSpec((tm, W), lambda j, i: (i, 0)) for _ in range(nb)]
                 + g_specs + [pl.BlockSpec((None, nb, W, tn), lambda j, i: (layer, 0, 0, j))],
        out_specs=pl.BlockSpec((tm, tn), lambda j, i: (i, j)),
        out_shape=jax.ShapeDtypeStruct((M, D), bf16),
        scratch_shapes=[pltpu.VMEM((nb, K, tn), bf16), pltpu.VMEM((nb, W, tn), bf16)],
        compiler_params=pltpu.CompilerParams(dimension_semantics=("parallel", "arbitrary"),
                                             vmem_limit_bytes=VMEM_LIMIT),
        name="merge")(h, *ys, *([w_all] * nb), w_branch)


def _rwkv_kernel(*refs, n_heads):
    T = RW_CHUNK
    dh = HEAD_DIM
    y_refs, h_ref = refs[12:14], refs[14]

    @pl.when(pl.program_id(1) == 0)
    def _():
        h_ref[...] = jnp.zeros_like(h_ref)

    row = lax.broadcasted_iota(jnp.int32, (T, T), 0)
    col = lax.broadcasted_iota(jnp.int32, (T, T), 1)
    eye = row == col
    zeros = jnp.zeros((T, dh), f32)
    chains = []
    for d in (0, 1):
        strict = row > col if d == 0 else row < col
        incl = strict | eye
        tri = jnp.where(incl, 1.0, 0.0).astype(bf16)
        lw, kk, v_all, aa, bb, rr = [r[0] for r in refs[6 * d:6 * d + 6]]
        c = _split_dot(tri, lw)
        c_end = c[T - 1:T, :] if d == 0 else c[0:1, :]
        e_neg = jnp.exp(-c)
        e_rem = jnp.exp(c_end - c)
        g_end = jnp.exp(c_end)
        at_all = aa * jnp.exp(c - lw)
        rt_all = rr * jnp.exp(c)
        bt_all = bb * e_neg
        kt_all = kk * e_neg
        bp_all = bb * e_rem
        kp_all = kk * e_rem
        for h in range(n_heads):
            s = slice(h * dh, (h + 1) * dh)
            chains.append((at_all[:, s], rt_all[:, s], bt_all[:, s], kt_all[:, s], bp_all[:, s], kp_all[:, s],
                           v_all[:, s], g_end[:, s], strict, incl))
    AT, RT, BT, KT, BP, KP, V, GE, ST, IN = zip(*chains)
    hd = range(len(chains))

    g = [_bdot_nt(jnp.concatenate([AT[i], RT[i]], axis=0), jnp.concatenate([BT[i], KT[i]], axis=0)) for i in hd]
    p = [jnp.where(ST[i], g[i][:T, :T], 0.0) for i in hd]
    u = [_bdot(jnp.where(ST[i], g[i][:T, T:], 0.0), V[i]) for i in hd]
    x = [jnp.concatenate([AT[i], u[i]], axis=1) for i in hd]
    n = 1
    while 2 * n < T:
        px = [_bdot(p[i], jnp.concatenate([x[i], p[i]], axis=1)) for i in hd]
        x = [x[i] + px[i][:, :2 * dh] for i in hd]
        p = [px[i][:, 2 * dh:] for i in hd]
        n *= 2
    x = [x[i] + _bdot(p[i], x[i]) for i in hd]
    z = [jnp.concatenate([x[i], jnp.concatenate([zeros, V[i]], axis=1)], axis=0) for i in hd]
    mx = [_bdot(jnp.concatenate([jnp.where(IN[i], g[i][T:, :T], 0.0), jnp.where(IN[i], g[i][T:, T:], 0.0)], axis=1),
                z[i]) for i in hd]
    bx = [_bdot_tn(jnp.concatenate([BP[i], KP[i]], axis=0), z[i]) for i in hd]
    yh = [_bdot(jnp.concatenate([RT[i] + mx[i][:, :dh], jnp.where(eye, GE[i], 0.0) + bx[i][:, :dh]], axis=0),
                h_ref[i]) for i in hd]
    for i in hd:
        d, h = divmod(i, n_heads)
        y_refs[d][0, :, h * dh:(h + 1) * dh] = yh[i][:T] + mx[i][:, dh:]
        h_ref[i] = yh[i][T:] + bx[i][:, dh:]


def _back_chunk(c, n_ctx, nc):
    return jnp.where(c < n_ctx, n_ctx - 1 - c, nc - 1 - c + n_ctx)


def _rwkv_scan(fwd, bwd, n_ctx):
    B, L, W = fwd[0].shape
    n_heads = W // HEAD_DIM
    nc = L // RW_CHUNK
    blk = (1, RW_CHUNK, W)
    f_spec = pl.BlockSpec(blk, lambda b, c: (b, c, 0))
    b_spec = pl.BlockSpec(blk, lambda b, c: (b, _back_chunk(c, n_ctx, nc), 0))
    return pl.pallas_call(
        functools.partial(_rwkv_kernel, n_heads=n_heads),
        grid=(B, nc),
        in_specs=[f_spec] * 6 + [b_spec] * 6,
        out_specs=[f_spec, b_spec],
        out_shape=[jax.ShapeDtypeStruct((B, L, W), f32)] * 2,
        scratch_shapes=[pltpu.VMEM((2 * n_heads, HEAD_DIM, HEAD_DIM), f32)],
        compiler_params=pltpu.CompilerParams(dimension_semantics=("parallel", "arbitrary"),
                                             vmem_limit_bytes=VMEM_LIMIT),
        name="rwkv")(*fwd, *bwd)


def _mlstm_kernel(*refs, n_heads, dirs):
    T = ML_CHUNK
    dh = HEAD_DIM
    nd_ = len(dirs)
    h_refs, c_ref, m_ref = refs[5 * nd_:6 * nd_], refs[6 * nd_], refs[6 * nd_ + 1]

    @pl.when(pl.program_id(1) == 0)
    def _():
        c_ref[...] = jnp.zeros_like(c_ref)
        m_ref[...] = jnp.zeros_like(m_ref)

    row = lax.broadcasted_iota(jnp.int32, (T, T), 0)
    col = lax.broadcasted_iota(jnp.int32, (T, T), 1)
    one_col = jnp.ones((T, dh), f32)
    chains = []
    for j, d in enumerate(dirs):
        q_ref, k_ref, v_ref, gc_ref, gr_ref = refs[5 * j:5 * j + 5]
        lower = row >= col if d == 0 else row <= col
        tri_l = jnp.where(lower, 1.0, 0.0).astype(bf16)
        tri_u = jnp.where(row <= col if d == 0 else row >= col, 1.0, 0.0).astype(bf16)
        gc = gc_ref[0]
        gr = gr_ref[0, 0]
        li_c = gc[:, :n_heads]
        b_c = _split_dot(tri_l, gc[:, n_heads:])
        li_r = gr[:n_heads, :]
        b_r = _split_dot_r(gr[n_heads:, :], tri_u)
        b_end = b_c[T - 1:T, :] if d == 0 else b_c[0:1, :]
        m_row = m_ref[j:j + 1, 0:n_heads]
        gcol = b_c + m_row
        run = _running_max(li_c - b_c, d)
        m_t = jnp.maximum(gcol, b_c + run)
        run_end = run[T - 1:T, :] if d == 0 else run[0:1, :]
        m_new = jnp.maximum(b_end + m_row, b_end + run_end)
        inter = jnp.exp(gcol - m_t)
        e_neg = jnp.exp(-m_t)
        bm = b_c - m_t
        carry = jnp.exp(b_end + m_row - m_new)
        kws = jnp.exp(b_end - b_c + li_c - m_new)
        m_ref[j:j + 1, 0:n_heads] = m_new
        q_all, k_all, v_all = q_ref[0], k_ref[0], v_ref[0]
        for h in range(n_heads):
            s = slice(h * dh, (h + 1) * dh)
            c1 = slice(h, h + 1)
            chains.append((q_all[:, s], k_all[:, s], jnp.concatenate([v_all[:, s], one_col], axis=1),
                           bm[:, c1], inter[:, c1], e_neg[:, c1], kws[:, c1], carry[:, c1],
                           b_r[c1, :], li_r[c1, :], lower))
    Q, K, VA, BM, INT, ENEG, KWS, CARRY, BR, LIR, LO = zip(*chains)
    hd = range(len(chains))
    state = [c_ref[i] for i in hd]

    dmat = [jnp.exp(jnp.where(LO[i], BM[i] - BR[i] + LIR[i], NEG_BIG)) for i in hd]
    qk = [_bdot_nt(Q[i], K[i]) for i in hd]
    nd = [_bdot(jnp.concatenate([INT[i] * Q[i], qk[i] * dmat[i]], axis=1),
                jnp.concatenate([state[i], VA[i]], axis=0)) for i in hd]
    upd = [_bdot_tn(K[i] * KWS[i], VA[i]) for i in hd]
    for i in hd:
        j, h = divmod(i, n_heads)
        h_refs[j][0, :, h * dh:(h + 1) * dh] = nd[i][:, :dh] / jnp.maximum(jnp.abs(nd[i][:, dh:]), ENEG[i])
        c_ref[i] = CARRY[i] * state[i] + upd[i]


def _mlstm_scan(q, k, v, li, lf, n_ctx):
    B, L, W = k.shape
    H = W // HEAD_DIM
    nc = L // ML_CHUNK
    gcol = jnp.concatenate([li, lf], axis=-1)
    grow = jnp.swapaxes(gcol.reshape(2 * B, nc, ML_CHUNK, 2 * H), 2, 3)

    def specs(d):
        def chunk(c):
            return c if d == 0 else _back_chunk(c, n_ctx, nc)
        seq = pl.BlockSpec((1, ML_CHUNK, W), lambda b, c: (b, chunk(c), 0))
        return seq, [seq] * 3 + [pl.BlockSpec((1, ML_CHUNK, 2 * H), lambda b, c: (b + d * B, chunk(c), 0)),
                                 pl.BlockSpec((1, 1, 2 * H, ML_CHUNK), lambda b, c: (b + d * B, chunk(c), 0, 0))]

    outs = []
    for d in (0, 1):
        out_spec, in_specs = specs(d)
        outs.append(pl.pallas_call(
            functools.partial(_mlstm_kernel, n_heads=H, dirs=(d,)),
            grid=(B, nc),
            in_specs=in_specs,
            out_specs=[out_spec],
            out_shape=[jax.ShapeDtypeStruct((B, L, W), f32)],
            scratch_shapes=[pltpu.VMEM((H, HEAD_DIM, 2 * HEAD_DIM), f32),
                            pltpu.VMEM((H, LANE), f32)],
            compiler_params=pltpu.CompilerParams(dimension_semantics=("parallel", "arbitrary"),
                                                 vmem_limit_bytes=VMEM_LIMIT),
            name="mlstm")(q, k, v, gcol, grow)[0])
    return outs


def _na_row_start(r, n_rows, win_r):
    return jnp.clip(r - win_r // 2, 0, n_rows - win_r)


def _na_kernel(qr_ref, qp_ref, k_ref, v_ref, kc_ref, vc_ref, bias_ref, o_ref, *, n_heads, n_rows, win_r):
    dh = HEAD_DIM
    r = pl.program_id(1)
    start = pl.multiple_of(_na_row_start(r, n_rows, win_r) * GRID_W, GRID_W)
    kwin = k_ref[0, pl.ds(start, win_r * GRID_W), :]
    vwin = v_ref[0, pl.ds(start, win_r * GRID_W), :]
    qr = qr_ref[0]
    qp = qp_ref[0]
    kc = kc_ref[0]
    vc = vc_ref[0]
    hd = range(n_heads)
    sls = [slice(h * dh, (h + 1) * dh) for h in hd]
    s_loc = [_bdot_nt(qr[:, sls[h]], kwin[:, sls[h]]) + bias_ref[h, 0] for h in hd]
    s_ctx = [_bdot_nt(qp[:, s], kc[:, s]) for s in sls]
    m = [jnp.maximum(jnp.max(s_loc[h], axis=1, keepdims=True), jnp.max(s_ctx[h], axis=1, keepdims=True)) for h in hd]
    p = [jnp.concatenate([jnp.exp(s_loc[h] - m[h]), jnp.exp(s_ctx[h] - m[h])], axis=1) for h in hd]
    o = [_bdot(p[h], jnp.concatenate([vwin[:, sls[h]], vc[:, sls[h]]], axis=0)) for h in hd]
    o_ref[0] = jnp.concatenate([o[h] / jnp.sum(p[h], axis=1, keepdims=True) for h in hd],
                               axis=1).astype(o_ref.dtype)


def _na_attention(q_rot, q_plain, k_rot, v, kc, vc, bias_tab, n_rows, win_r):
    B, S, W = q_rot.shape
    Lc = kc.shape[1]
    H = W // HEAD_DIM
    n_var = bias_tab.shape[1]
    qspec = pl.BlockSpec((1, GRID_W, W), lambda b, r: (b, r, 0))
    full = pl.BlockSpec((1, S, W), lambda b, r: (b, 0, 0))
    cspec = pl.BlockSpec((1, Lc, W), lambda b, r: (b, 0, 0))

    def bias_map(b, r):
        return (0, _na_row_start(r, n_rows, win_r) - r + win_r - 1, 0, 0)

    return pl.pallas_call(
        functools.partial(_na_kernel, n_heads=H, n_rows=n_rows, win_r=win_r),
        grid=(B, n_rows),
        in_specs=[qspec, qspec, full, full, cspec, cspec,
                  pl.BlockSpec((H, 1, GRID_W, win_r * GRID_W), bias_map)],
        out_specs=qspec,
        out_shape=jax.ShapeDtypeStruct((B, S, W), bf16),
        compiler_params=pltpu.CompilerParams(dimension_semantics=("parallel", "arbitrary"),
                                             vmem_limit_bytes=VMEM_LIMIT),
        name="na")(q_rot, q_plain, k_rot, v, kc, vc, bias_tab)


def _ctx_attn_kernel(q_ref, k_ref, v_ref, o_ref, *, n_heads):
    dh = HEAD_DIM
    q = q_ref[0]
    k = k_ref[0]
    v = v_ref[0]
    for h in range(n_heads):
        sl = slice(h * dh, (h + 1) * dh)
        s = _bdot_nt(q[:, sl], k[:, sl])
        m = jnp.max(s, axis=1, keepdims=True)
        p = jnp.exp(s - m)
        o_ref[0, :, sl] = _bdot(p, v[:, sl]) / jnp.sum(p, axis=1, keepdims=True)


def _ctx_attention(q, k, v):
    B, Lc, W = q.shape
    spec = pl.BlockSpec((1, Lc, W), lambda b: (b, 0, 0))
    return pl.pallas_call(
        functools.partial(_ctx_attn_kernel, n_heads=W // HEAD_DIM),
        grid=(B,), in_specs=[spec] * 3, out_specs=spec,
        out_shape=jax.ShapeDtypeStruct((B, Lc, W), f32),
        compiler_params=pltpu.CompilerParams(dimension_semantics=("parallel",), vmem_limit_bytes=VMEM_LIMIT),
        name="ctx_attn")(q, k, v)


def _moe_kernel(be_ref, nused_ref, x_ref, wgu_ref, bgu_ref, wdn_ref, bdn_ref, prev_ref, o_ref,
                wgu_scr, wdn_scr, *, d_expert):
    del prev_ref
    i = pl.program_id(0)

    @pl.when((i == 0) | (be_ref[i] != be_ref[jnp.maximum(i - 1, 0)]))
    def _():
        wgu_scr[...] = wgu_ref[0].astype(bf16)
        wdn_scr[...] = wdn_ref[0].astype(bf16)

    @pl.when(i < nused_ref[0])
    def _():
        gu = jnp.dot(x_ref[...], wgu_scr[...], preferred_element_type=f32) + bgu_ref[0]
        gt = jnp.minimum(gu[:, :d_expert], SWIGLU_LIMIT)
        up = jnp.clip(gu[:, d_expert:], -SWIGLU_LIMIT, SWIGLU_LIMIT)
        act = (up + 1.0) * (gt / (1.0 + jnp.exp(-SWIGLU_ALPHA * gt)))
        y = jnp.dot(act.astype(bf16), wdn_scr[...], preferred_element_type=f32) + bdn_ref[0]
        o_ref[...] = y.astype(o_ref.dtype)

    @pl.when(i >= nused_ref[0])
    def _():
        o_ref[...] = jnp.zeros_like(o_ref)


def _moe_experts(xg, block_e, n_used, w_gu, b_gu, w_dn, b_dn, layer, ys, blk_off):
    R, D = xg.shape
    n_blocks = R // MOE_BLOCK
    F2 = w_gu.shape[-1]
    grid_spec = pltpu.PrefetchScalarGridSpec(
        num_scalar_prefetch=2,
        grid=(n_blocks,),
        in_specs=[pl.BlockSpec((MOE_BLOCK, D), lambda i, be, nu: (i, 0)),
                  pl.BlockSpec((None, 1, D, F2), lambda i, be, nu: (layer, be[i], 0, 0)),
                  pl.BlockSpec((None, 1, 1, F2), lambda i, be, nu: (layer, be[i], 0, 0)),
                  pl.BlockSpec((None, 1, F2 // 2, D), lambda i, be, nu: (layer, be[i], 0, 0)),
                  pl.BlockSpec((None, 1, 1, D), lambda i, be, nu: (layer, be[i], 0, 0)),
                  pl.BlockSpec(memory_space=pl.ANY)],
        out_specs=pl.BlockSpec((MOE_BLOCK, D), lambda i, be, nu: (i + blk_off, 0)),
        scratch_shapes=[pltpu.VMEM((D, F2), bf16), pltpu.VMEM((F2 // 2, D), bf16)])
    return pl.pallas_call(
        functools.partial(_moe_kernel, d_expert=F2 // 2),
        grid_spec=grid_spec,
        out_shape=jax.ShapeDtypeStruct(ys.shape, ys.dtype),
        input_output_aliases={7: 0},
        compiler_params=pltpu.CompilerParams(dimension_semantics=("arbitrary",), vmem_limit_bytes=VMEM_LIMIT),
        name="moe")(block_e, n_used, xg, w_gu, b_gu[:, :, None, :], w_dn, b_dn[:, :, None, :], ys)


def _moe(h_hi, h_lo, w_router, b_router, w_gu, b_gu, w_dn, b_dn, layer):
    T, D = h_hi.shape
    E = w_router.shape[1]
    A = T * TOP_K
    w_hi = w_router.astype(bf16)
    w_lo = (w_router - w_hi.astype(f32)).astype(bf16)
    logits = _mm(h_hi, w_hi) + _mm(h_lo, w_hi) + _mm(h_hi, w_lo) + b_router
    top_val, top_idx = lax.top_k(logits, TOP_K)
    gate = jax.nn.softmax(top_val, axis=-1)
    flat_e = top_idx.reshape(-1).astype(jnp.int32)
    onehot = flat_e[:, None] == jnp.arange(E, dtype=jnp.int32)
    counts = jnp.sum(onehot.astype(jnp.int32), axis=0)
    padded = (counts + MOE_BLOCK - 1) // MOE_BLOCK * MOE_BLOCK
    pad_end = jnp.cumsum(padded)
    pad_start = pad_end - padded
    start = jnp.cumsum(counts) - counts
    order = jnp.argsort(flat_e).astype(jnp.int32)
    inv = jnp.argsort(order).astype(jnp.int32)
    shift_e = pad_start - start
    pos = (inv + jnp.sum(jnp.where(onehot, shift_e, 0), axis=1)).reshape(T, TOP_K)
    n_blocks = (A + MOE_BLOCK - 1) // MOE_BLOCK + E
    blk_first = jnp.arange(n_blocks, dtype=jnp.int32) * MOE_BLOCK
    block_e = jnp.minimum(jnp.sum((pad_end[None, :] <= blk_first[:, None]).astype(jnp.int32), axis=1), E - 1)
    n_used = (pad_end[-1:] // MOE_BLOCK).astype(jnp.int32)
    slot = jnp.arange(n_blocks * MOE_BLOCK, dtype=jnp.int32).reshape(n_blocks, MOE_BLOCK)
    off_in = slot - pad_start[block_e][:, None]
    valid = off_in < counts[block_e][:, None]
    srt = jnp.clip(start[block_e][:, None] + off_in, 0, A - 1)
    tok_buf = jnp.where(valid, order[srt.reshape(-1)].reshape(srt.shape) // TOP_K, 0).reshape(-1)
    ys = jnp.zeros((n_blocks * MOE_BLOCK, D), bf16)
    per = -(-n_blocks // MOE_SPLIT)
    for q in range(MOE_SPLIT):
        lo, hi = q * per, min((q + 1) * per, n_blocks)
        if lo >= hi:
            break
        xg = h_hi[tok_buf[lo * MOE_BLOCK:hi * MOE_BLOCK]]
        ys = _moe_experts(xg, block_e[lo:hi], n_used - lo, w_gu, b_gu, w_dn, b_dn, layer, ys, lo)
    out = gate[:, 0:1] * ys[pos[:, 0]]
    for j in range(1, TOP_K):
        out = out + gate[:, j:j + 1] * ys[pos[:, j]]
    return out


def _rms_norm(x, g):
    return x * lax.rsqrt(jnp.mean(x * x, -1, keepdims=True) + NORM_EPS) * g


def _group_norm(y, g, b, n_heads, eps):
    shp = y.shape
    yh = y.reshape(shp[:-1] + (n_heads, shp[-1] // n_heads))
    mu = jnp.mean(yh, -1, keepdims=True)
    var = jnp.mean(jnp.square(yh - mu), -1, keepdims=True)
    out = ((yh - mu) * lax.rsqrt(var + eps)).reshape(shp) * g
    return out if b is None else out + b


def _dft_mats(n, scale):
    j = jnp.arange(n, dtype=jnp.int32)
    m = (j[:, None] * j[None, :]) % n
    ang = m.astype(f32) * (2.0 * math.pi / n)
    return jnp.cos(ang) * scale, jnp.sin(ang) * scale


def _axial_rope(t, rows, cols):
    half = HEAD_DIM // 2
    nf = half // 2
    inv = ROPE_THETA ** (-jnp.arange(nf, dtype=f32) / nf)

    def rot(u, pos):
        ang = pos.astype(f32)[:, None] * inv
        cos = jnp.cos(ang)[None, :, None, :]
        sin = jnp.sin(ang)[None, :, None, :]
        u1, u2 = u[..., :nf], u[..., nf:]
        return jnp.concatenate([u1 * cos - u2 * sin, u1 * sin + u2 * cos], -1)

    return jnp.concatenate([rot(t[..., :half], rows), rot(t[..., half:], cols)], -1)


def _seg_map(fn, u, Lc):
    return jnp.concatenate([fn(u[:, :Lc]), fn(u[:, Lc:])], axis=1)


def _shift(u, direction):
    if direction == 0:
        return jnp.pad(u[:, :-1], ((0, 0), (1, 0), (0, 0)))
    return jnp.pad(u[:, 1:], ((0, 0), (0, 1), (0, 0)))


def _conv3(u, w):
    return w[0] * _shift(u, 0) + w[1] * u + w[2] * _shift(u, 1)


def _na_bias_table(rpb, win_r):
    H = rpb.shape[0]
    w = jnp.arange(GRID_W)
    col_start = jnp.clip(w - NA_WIN_COLS // 2, 0, GRID_W - NA_WIN_COLS)
    col_in = (w[None, :] >= col_start[:, None]) & (w[None, :] < col_start[:, None] + NA_WIN_COLS)
    col_off = jnp.clip(w[None, :] - w[:, None], 1 - NA_WIN_COLS, NA_WIN_COLS - 1) + NA_WIN_COLS - 1
    per_row = jnp.where(col_in[None, None], rpb[:, :, col_off], NEG_BIG)
    tabs = []
    for j in range(win_r):
        d0 = j - (win_r - 1)
        rows = [per_row[:, d0 + a + NA_WIN_ROWS - 1] for a in range(win_r)]
        tabs.append(jnp.concatenate(rows, axis=-1))
    return jnp.stack(tabs, axis=1)


def _mixer(hcat, lp, B, Lc, S, last):
    D = hcat.shape[-1]
    BW = D // N_BRANCHES
    H = BW // HEAD_DIM
    Lt = Lc + S
    M = B * Lt
    w_in = lp['w_in']
    names = ('na_k', 'na_v', 'rw_k', 'rw_v', 'rw_wl', 'rw_al', 'ml_k', 'ml_v', 'ml_g',
             'fo', 'na_q', 'rw_r', 'rw_gl', 'ml_q', 'ml_o', 'gates')
    sizes = (BW, BW, BW, BW, 2 * RW_DECAY_LORA, 2 * RW_ICL_LORA, BW, BW, 4 * H,
             BW, BW, BW, RW_GATE_LORA, BW, BW, N_BRANCHES * D)
    src, o = {}, 0
    for nm, sz in zip(names, sizes):
        src[nm] = (o, sz)
        o += sz
    groups = (('na_k', 'na_v', 'na_q'), ('rw_k', 'rw_v', 'rw_r', 'rw_wl', 'rw_al', 'rw_gl'),
              ('ml_k', 'ml_v', 'ml_q', 'ml_o', 'ml_g'), ('fo',))
    cols, dst, o = [], {}, 0
    for grp in groups:
        for nm in grp:
            a, sz = src[nm]
            cols.append(w_in[:, a:a + sz])
            dst[nm] = (o, sz)
            o += sz
        pad = (-o) % BW
        if pad:
            cols.append(jnp.zeros((D, pad), w_in.dtype))
            o += pad
    gate_off = o
    a, sz = src['gates']
    w2 = jnp.concatenate(cols + [w_in[:, a:a + sz]], axis=1)
    h2d = hcat.reshape(M, D).astype(bf16)
    px = _mm_w(h2d, w2, 0, gate_off, out_dtype=bf16).reshape(B, Lt, gate_off)

    def part(nm):
        a_, sz_ = dst[nm]
        return px[..., a_:a_ + sz_].astype(f32)

    na_k, na_v, na_q = part('na_k'), part('na_v'), part('na_q')
    rw_k, rw_v, rw_r = part('rw_k'), part('rw_v'), part('rw_r')
    rw_wl, rw_al, rw_gl = part('rw_wl'), part('rw_al'), part('rw_gl')
    ml_k, ml_v, ml_q, ml_o, ml_g = part('ml_k'), part('ml_v'), part('ml_q'), part('ml_o'), part('ml_g')
    fo = part('fo')

    C = BW // FOURIER_GROUPS
    cc, sc = _dft_mats(C, C ** -0.5)
    eye_g = jnp.eye(FOURIER_GROUPS, dtype=f32)
    chan = jnp.concatenate([jnp.kron(eye_g, cc), -jnp.kron(eye_g, sc)], axis=1)
    uc = _mm(fo.reshape(M, BW), chan).reshape(B, Lt, 2 * BW)

    def pos_dft(u, L):
        cl, sl_ = _dft_mats(L, L ** -0.5)
        rhs = jnp.concatenate([jnp.swapaxes(u[..., :BW], 0, 1).reshape(L, B * BW),
                               jnp.swapaxes(u[..., BW:], 0, 1).reshape(L, B * BW)], axis=0)
        y = _mm(jnp.concatenate([cl, sl_], axis=1), rhs, out_dtype=bf16)
        return jnp.swapaxes(y.reshape(L, B, BW), 0, 1)

    y_four = jnp.concatenate([pos_dft(uc[:, :Lc], Lc), pos_dft(uc[:, Lc:], S)], axis=1)

    n_rows = S // GRID_W
    win_r = min(NA_WIN_ROWS, n_rows)
    scale = HEAD_DIM ** -0.5
    t = jnp.arange(S)
    q_lat = na_q[:, Lc:].reshape(B, S, H, HEAD_DIM)
    k_lat = na_k[:, Lc:].reshape(B, S, H, HEAD_DIM)
    q_rot = (_axial_rope(q_lat, t // GRID_W, t % GRID_W) * scale).reshape(B, S, BW).astype(bf16)
    k_rot = _axial_rope(k_lat, t // GRID_W, t % GRID_W).reshape(B, S, BW).astype(bf16)
    q_plain = (na_q[:, Lc:] * scale).astype(bf16)
    kc = na_k[:, :Lc].astype(bf16)
    vc = na_v[:, :Lc].astype(bf16)
    bias_tab = _na_bias_table(lp['na_rpb'], win_r)
    y_na_x = _na_attention(q_rot, q_plain, k_rot, na_v[:, Lc:].astype(bf16), kc, vc, bias_tab, n_rows, win_r)
    if last:
        y_na_c = jnp.zeros((B, Lc, BW), bf16)
    else:
        y_na_c = _ctx_attention((na_q[:, :Lc] * scale).astype(bf16), kc, vc).astype(bf16)
    y_na = jnp.concatenate([y_na_c, y_na_x], axis=1)

    def flip_seg(u):
        return _seg_map(lambda s: jnp.flip(s, axis=1), u, Lc)

    seqs = []
    bonus_in = []
    for d in (0, 1):
        mu = lp['rw_mu_rkv'][d]
        mu_wa = lp['rw_mu_wa'][d]

        def lerp(u, m_):
            return u + m_ * (_seg_map(lambda s: _shift(s, d), u, Lc) - u)

        r_l = lerp(rw_r, mu[:BW])
        k_l = lerp(rw_k, mu[BW:2 * BW])
        v_l = lerp(rw_v, mu[2 * BW:])
        wl_l = lerp(rw_wl[..., d * RW_DECAY_LORA:(d + 1) * RW_DECAY_LORA], mu_wa[:RW_DECAY_LORA])
        al_l = lerp(rw_al[..., d * RW_ICL_LORA:(d + 1) * RW_ICL_LORA], mu_wa[RW_DECAY_LORA:])
        w_pre = lp['rw_w0'][d] + _mm_w(jnp.tanh(wl_l).reshape(M, -1), lp['rw_w_up'][d]).reshape(B, Lt, BW)
        w_log = -jax.nn.softplus(-w_pre) - 0.5
        lw = -jnp.exp(w_log)
        a_icl = jax.nn.sigmoid(lp['rw_a0'][d] + _mm_w(al_l.reshape(M, -1), lp['rw_a_up'][d]).reshape(B, Lt, BW))
        kk = (k_l * lp['rw_k_k']).reshape(B, Lt, H, HEAD_DIM)
        kk = (kk / jnp.maximum(jnp.sqrt(jnp.sum(kk * kk, -1, keepdims=True)), 1e-12)).reshape(B, Lt, BW)
        k_mod = k_l * (1.0 + (a_icl - 1.0) * lp['rw_k_a'])
        seqs.append((lw, k_mod, v_l, -kk, kk * a_icl, r_l))
        bonus_in.append((r_l, k_mod, v_l))
    ys = _rwkv_scan(seqs[0], seqs[1], n_ctx=Lc // RW_CHUNK)
    y_rw = 0.0
    for d in (0, 1):
        y_d = _group_norm(ys[d], lp['rw_gn_g'][d], lp['rw_gn_b'][d], H, RW_GN_EPS)
        r_l, k_mod, v_l = bonus_in[d]
        hs = (B, Lt, H, HEAD_DIM)
        bonus = jnp.sum(r_l.reshape(hs) * k_mod.reshape(hs) * lp['rw_r_k'][d], -1, keepdims=True) * v_l.reshape(hs)
        y_rw = y_rw + y_d + bonus.reshape(B, Lt, BW)
    y_rw = y_rw * _mm_w(jax.nn.sigmoid(rw_gl).reshape(M, -1), lp['rw_g_up']).reshape(B, Lt, BW)

    k_m = jax.nn.silu(_seg_map(lambda s: _conv3(s, lp['ml_conv_k']), ml_k, Lc)) * scale
    q_m = jax.nn.silu(_seg_map(lambda s: _conv3(s, lp['ml_conv_q']), ml_q, Lc))
    g4 = ml_g.reshape(B, Lt, 4, H) + lp['ml_gate_b']
    li = g4[:, :, :2]
    lf = jax.nn.log_sigmoid(g4[:, :, 2:])
    li_s = jnp.concatenate([li[:, :, 0], li[:, :, 1]], axis=0)
    lf_s = jnp.concatenate([lf[:, :, 0], lf[:, :, 1]], axis=0)
    hs_ = _mlstm_scan(q_m, k_m, ml_v, li_s, lf_s, n_ctx=Lc // ML_CHUNK)
    h_sum = hs_[0] + hs_[1]
    y_ml = _group_norm(h_sum, lp['ml_gn_g'], None, H, GN_EPS) * jax.nn.sigmoid(ml_o)

    ys4 = [u.reshape(M, BW).astype(bf16) for u in (y_four, y_na, y_rw, y_ml)]
    acc = _merge(h2d, ys4, w2, gate_off, lp['w_branch'], lp['layer'])
    return _mm_w(acc, lp['w_out'], lead=(lp['layer'],)).reshape(B, Lt, D)


def kernel(x, c, ctx, c_ctx, w_ada, b_ada, g_norm1, g_norm2, w_in, na_rpb, rw_mu_rkv, rw_mu_wa, rw_w_up, rw_w0, rw_a_up, rw_a0, rw_g_up, rw_k_k, rw_k_a, rw_r_k, rw_gn_g, rw_gn_b, ml_conv_q, ml_conv_k, ml_gate_b, ml_gn_g, w_branch, w_out, w_router, b_router, w_gu, b_gu, w_dn, b_dn, g_final):
    B, S, D = x.shape
    Lc = ctx.shape[1]
    depth = w_ada.shape[0]
    mod_in = jnp.concatenate([jax.nn.silu(c), jax.nn.silu(c_ctx)[None]], axis=0)
    mod_in = jnp.pad(mod_in, ((0, (-(B + 1)) % 16), (0, 0)))
    Lt = Lc + S
    pending = None
    for l in range(depth):
        last = l == depth - 1
        lp = {'w_in': w_in[l], 'na_rpb': na_rpb[l], 'rw_mu_rkv': rw_mu_rkv[l], 'rw_mu_wa': rw_mu_wa[l],
              'rw_w_up': rw_w_up[l], 'rw_w0': rw_w0[l], 'rw_a_up': rw_a_up[l], 'rw_a0': rw_a0[l],
              'rw_g_up': rw_g_up[l], 'rw_k_k': rw_k_k[l], 'rw_k_a': rw_k_a[l], 'rw_r_k': rw_r_k[l],
              'rw_gn_g': rw_gn_g[l], 'rw_gn_b': rw_gn_b[l], 'ml_conv_q': ml_conv_q[l],
              'ml_conv_k': ml_conv_k[l], 'ml_gate_b': ml_gate_b[l], 'ml_gn_g': ml_gn_g[l],
              'w_branch': w_branch, 'w_out': w_out, 'layer': l}
        mod = _mm_w(mod_in, w_ada, lead=(l,)) + b_ada[l]
        mx = jnp.split(mod[:B], 6, axis=-1)
        mc = jnp.split(mod[B], 6, axis=-1)
        if pending is None:
            hx, _ = _norm_mod(x, g_norm1[l], mx[1], mx[0])
        else:
            x, hx, _ = _norm_mod(x, g_norm1[l], mx[1], mx[0], res=pending)
        hc = _rms_norm(ctx, g_norm1[l]) * (1.0 + mc[1]) + mc[0]
        y = _mixer(jnp.concatenate([hc.astype(bf16), hx], axis=1), lp, B, Lc, S, last)
        x, hx2_hi, hx2_lo = _norm_mod(x, g_norm2[l], mx[4], mx[3], res=(y.reshape(B * Lt, D), Lc, Lt, mx[2]))
        if last:
            ym = _moe(hx2_hi.reshape(-1, D), hx2_lo.reshape(-1, D), w_router[l], b_router[l],
                      w_gu, b_gu, w_dn, b_dn, l)
        else:
            ctx = ctx + mc[2] * y[:, :Lc]
            hc2 = (_rms_norm(ctx, g_norm2[l]) * (1.0 + mc[4]) + mc[3]).reshape(-1, D)
            hc2_hi = hc2.astype(bf16)
            n_x = B * S
            ym = _moe(jnp.concatenate([hx2_hi.reshape(-1, D), hc2_hi], axis=0),
                      jnp.concatenate([hx2_lo.reshape(-1, D), (hc2 - hc2_hi.astype(f32)).astype(bf16)], axis=0),
                      w_router[l], b_router[l], w_gu, b_gu, w_dn, b_dn, l)
            ctx = ctx + mc[5] * ym[n_x:].reshape(ctx.shape)
        pending = (ym, 0, S, mx[5])
    zero = jnp.zeros((B, D), f32)
    return _norm_mod(x, g_final, zero, zero, res=pending, emit_x=False, split=False)[0]
```

```python
import functools
import math

import jax
import jax.numpy as jnp
from jax import lax
from jax.experimental import pallas as pl
from jax.experimental.pallas import tpu as pltpu

f32 = jnp.float32
bf16 = jnp.bfloat16

HEAD_DIM = 64
GRID_W = 64
N_BRANCHES = 4
FOURIER_GROUPS = 4
NA_WIN_ROWS = 8
NA_WIN_COLS = 16
ROPE_THETA = 10000.0
RW_DECAY_LORA = 64
RW_ICL_LORA = 64
RW_GATE_LORA = 128
RW_GN_EPS = 64e-5
ML_CHUNK = 64
RW_CHUNK = 64
TOP_K = 4
SWIGLU_LIMIT = 7.0
SWIGLU_ALPHA = 1.702
MOE_BLOCK = 256
MOE_SPLIT = 4
NORM_EPS = 1e-6
GN_EPS = 1e-5
NEG_BIG = -1e30

LANE = 128
VMEM_LIMIT = 56 * 1024 * 1024


def _bdot(a, b):
    return jnp.dot(a.astype(bf16), b.astype(bf16), preferred_element_type=f32)


def _bdot_nt(a, b):
    return lax.dot_general(a.astype(bf16), b.astype(bf16), (((1,), (1,)), ((), ())),
                           preferred_element_type=f32)


def _bdot_tn(a, b):
    return lax.dot_general(a.astype(bf16), b.astype(bf16), (((0,), (0,)), ((), ())),
                           preferred_element_type=f32)


def _split_dot(tri, x):
    hi = x.astype(bf16)
    lo = (x - hi.astype(f32)).astype(bf16)
    return (jnp.dot(tri, hi, preferred_element_type=f32) + jnp.dot(tri, lo, preferred_element_type=f32))


def _split_dot_r(x, tri):
    hi = x.astype(bf16)
    lo = (x - hi.astype(f32)).astype(bf16)
    return (jnp.dot(hi, tri, preferred_element_type=f32) + jnp.dot(lo, tri, preferred_element_type=f32))


def _running_max(x, d):
    T = x.shape[0]
    t = lax.broadcasted_iota(jnp.int32, x.shape, 0)
    s = 1
    while s < T:
        if d == 0:
            x = jnp.maximum(x, jnp.where(t >= s, pltpu.roll(x, s, axis=0), NEG_BIG))
        else:
            x = jnp.maximum(x, jnp.where(t < T - s, pltpu.roll(x, T - s, axis=0), NEG_BIG))
        s *= 2
    return x


def _pick(dim, cap, align):
    if dim <= cap:
        return dim
    t = cap - cap % align
    while t >= align:
        if dim % t == 0:
            return t
        t -= align
    return dim


def _mm_kernel_single(a_ref, b_ref, o_ref):
    o_ref[...] = jnp.dot(a_ref[...], b_ref[...], preferred_element_type=f32).astype(o_ref.dtype)


def _mm_kernel_acc(a_ref, b_ref, o_ref, acc_ref, *, nk):
    k = pl.program_id(2)

    @pl.when(k == 0)
    def _():
        acc_ref[...] = jnp.zeros_like(acc_ref)

    acc_ref[...] += jnp.dot(a_ref[...], b_ref[...], preferred_element_type=f32)

    @pl.when(k == nk - 1)
    def _():
        o_ref[...] = acc_ref[...].astype(o_ref.dtype)


def _mm(a, b, out_dtype=f32):
    a = a.astype(bf16)
    b = b.astype(bf16)
    M, K = a.shape
    N = b.shape[1]
    tm = _pick(M, 1024, 16)
    tn = _pick(N, 1024, LANE)
    tk = _pick(K, 2048, LANE)
    nk = K // tk
    grid = (M // tm, N // tn, nk)
    in_specs = [pl.BlockSpec((tm, tk), lambda i, j, k: (i, k)),
                pl.BlockSpec((tk, tn), lambda i, j, k: (k, j))]
    out_spec = pl.BlockSpec((tm, tn), lambda i, j, k: (i, j))
    params = pltpu.CompilerParams(dimension_semantics=("parallel", "parallel", "arbitrary"),
                                  vmem_limit_bytes=VMEM_LIMIT)
    if nk == 1:
        return pl.pallas_call(_mm_kernel_single, grid=grid, in_specs=in_specs, out_specs=out_spec,
                              out_shape=jax.ShapeDtypeStruct((M, N), out_dtype),
                              compiler_params=params, name="mm")(a, b)
    return pl.pallas_call(functools.partial(_mm_kernel_acc, nk=nk), grid=grid, in_specs=in_specs,
                          out_specs=out_spec, out_shape=jax.ShapeDtypeStruct((M, N), out_dtype),
                          scratch_shapes=[pltpu.VMEM((tm, tn), f32)],
                          compiler_params=params, name="mm_acc")(a, b)


def _mm_ws_kernel(a_ref, b_ref, o_ref, w_ref):
    @pl.when(pl.program_id(1) == 0)
    def _():
        w_ref[...] = b_ref[...].astype(bf16)

    o_ref[...] = jnp.dot(a_ref[...], w_ref[...], preferred_element_type=f32).astype(o_ref.dtype)


def _mm_w(a, w, col_off=0, n_cols=None, out_dtype=f32, lead=()):
    a = a.astype(bf16)
    M, K = a.shape
    n_cols = w.shape[-1] - col_off if n_cols is None else n_cols
    tm = _pick(M, 1024, 16)
    tn = _pick(n_cols, 512, LANE)
    if col_off % tn:
        tn = math.gcd(tn, col_off)
    off = col_off // tn
    return pl.pallas_call(
        _mm_ws_kernel,
        grid=(n_cols // tn, M // tm),
        in_specs=[pl.BlockSpec((tm, K), lambda j, i: (i, 0)),
                  pl.BlockSpec((None,) * len(lead) + (K, tn), lambda j, i: tuple(lead) + (0, j + off))],
        out_specs=pl.BlockSpec((tm, tn), lambda j, i: (i, j)),
        out_shape=jax.ShapeDtypeStruct((M, n_cols), out_dtype),
        scratch_shapes=[pltpu.VMEM((K, tn), bf16)],
        compiler_params=pltpu.CompilerParams(dimension_semantics=("parallel", "arbitrary"),
                                             vmem_limit_bytes=VMEM_LIMIT),
        name="mm_w")(a, w)


def _norm_kernel(*refs, has_res, emit_x, split):
    x = refs[0][0]
    n_in = 1
    if has_res:
        x = x + refs[2][0] * refs[1][...]
        n_in = 3
    g, scale, shift = refs[n_in][...], refs[n_in + 1][0], refs[n_in + 2][0]
    outs = refs[n_in + 3:]
    h = x * lax.rsqrt(jnp.mean(x * x, axis=-1, keepdims=True) + NORM_EPS) * g * (1.0 + scale) + shift
    if emit_x:
        outs[0][0] = x
        outs = outs[1:]
    if split:
        hi = h.astype(bf16)
        outs[0][0] = hi
        outs[1][0] = (h - hi.astype(f32)).astype(bf16)
    else:
        outs[0][0] = h


def _norm_mod(x, g, scale, shift, res=None, emit_x=True, split=True):
    B, S, D = x.shape
    tm = math.gcd(S, 256)
    row = pl.BlockSpec((1, tm, D), lambda b, i: (b, i, 0))
    vec = pl.BlockSpec((1, 1, D), lambda b, i: (b, 0, 0))
    args, in_specs = [x], [row]
    if res is not None:
        y2d, row0, row_stride, gate = res
        tm = math.gcd(math.gcd(tm, row_stride), row0) if row0 else math.gcd(tm, row_stride)
        row = pl.BlockSpec((1, tm, D), lambda b, i: (b, i, 0))
        args += [y2d, gate[:, None, :]]
        in_specs = [row, pl.BlockSpec((tm, D), lambda b, i: ((row0 + b * row_stride) // tm + i, 0)), vec]
    args += [g[None, :], scale[:, None, :], shift[:, None, :]]
    in_specs += [pl.BlockSpec((1, D), lambda b, i: (0, 0)), vec, vec]
    emit_x = emit_x and res is not None
    out_shape, out_specs = [], []
    if emit_x:
        out_shape.append(jax.ShapeDtypeStruct((B, S, D), f32))
        out_specs.append(row)
    for dt in ((bf16, bf16) if split else (f32,)):
        out_shape.append(jax.ShapeDtypeStruct((B, S, D), dt))
        out_specs.append(row)
    return pl.pallas_call(
        functools.partial(_norm_kernel, has_res=res is not None, emit_x=emit_x, split=split),
        grid=(B, S // tm), in_specs=in_specs, out_specs=out_specs, out_shape=out_shape,
        compiler_params=pltpu.CompilerParams(dimension_semantics=("parallel", "parallel"),
                                             vmem_limit_bytes=VMEM_LIMIT),
        name="norm_mod")(*args)


def _merge_kernel(*refs, nb):
    h_ref = refs[0]
    y_refs, g_refs = refs[1:1 + nb], refs[1 + nb:1 + 2 * nb]
    wb_ref, o_ref, wg_scr, wb_scr = refs[1 + 2 * nb:]

    @pl.when(pl.program_id(1) == 0)
    def _():
        for i in range(nb):
            wg_scr[i] = g_refs[i][...].astype(bf16)
        wb_scr[...] = wb_ref[...].astype(bf16)

    h = h_ref[...]
    acc = None
    for i in range(nb):
        gate = jnp.dot(h, wg_scr[i], preferred_element_type=f32)
        p = jnp.dot(y_refs[i][...], wb_scr[i], preferred_element_type=f32)
        t = p / (1.0 + jnp.exp(-gate))
        acc = t if acc is None else acc + t
    o_ref[...] = acc.astype(o_ref.dtype)


def _merge(h, ys, w_all, gate_off, w_branch, layer):
    nb = len(ys)
    M, K = h.shape
    W = ys[0].shape[1]
    D = w_branch.shape[-1]
    tm = _pick(M, 1024, 16)
    tn = math.gcd(_pick(D, 256, LANE), gate_off)
    nj = D // tn
    off = gate_off // tn
    g_specs = [pl.BlockSpec((K, tn), functools.partial(lambda j, i, b: (0, off + b * nj + j), b=b))
               for b in range(nb)]
    return pl.pallas_call(
        functools.partial(_merge_kernel, nb=nb),
        grid=(nj, M // tm),
        in_specs=[pl.BlockSpec((tm, K), lambda j, i: (i, 0))]
                 + [pl.BlockSpec((tm, W), lambda j, i: (i, 0)) for _ in range(nb)]
                 + g_specs + [pl.BlockSpec((None, nb, W, tn), lambda j, i: (layer, 0, 0, j))],
        out_specs=pl.BlockSpec((tm, tn), lambda j, i: (i, j)),
        out_shape=jax.ShapeDtypeStruct((M, D), bf16),
        scratch_shapes=[pltpu.VMEM((nb, K, tn), bf16), pltpu.VMEM((nb, W, tn), bf16)],
        compiler_params=pltpu.CompilerParams(dimension_semantics=("parallel", "arbitrary"),
                                             vmem_limit_bytes=VMEM_LIMIT),
        name="merge")(h, *ys, *([w_all] * nb), w_branch)


def _rwkv_kernel(*refs, n_heads):
    T = RW_CHUNK
    dh = HEAD_DIM
    y_refs, h_ref = refs[12:14], refs[14]

    @pl.when(pl.program_id(1) == 0)
    def _():
        h_ref[...] = jnp.zeros_like(h_ref)

    row = lax.broadcasted_iota(jnp.int32, (T, T), 0)
    col = lax.broadcasted_iota(jnp.int32, (T, T), 1)
    eye = row == col
    zeros = jnp.zeros((T, dh), f32)
    chains = []
    for d in (0, 1):
        strict = row > col if d == 0 else row < col
        incl = strict | eye
        tri = jnp.where(incl, 1.0, 0.0).astype(bf16)
        lw, kk, v_all, aa, bb, rr = [r[0] for r in refs[6 * d:6 * d + 6]]
        c = _split_dot(tri, lw)
        c_end = c[T - 1:T, :] if d == 0 else c[0:1, :]
        e_neg = jnp.exp(-c)
        e_rem = jnp.exp(c_end - c)
        g_end = jnp.exp(c_end)
        at_all = aa * jnp.exp(c - lw)
        rt_all = rr * jnp.exp(c)
        bt_all = bb * e_neg
        kt_all = kk * e_neg
        bp_all = bb * e_rem
        kp_all = kk * e_rem
        for h in range(n_heads):
            s = slice(h * dh, (h + 1) * dh)
            chains.append((at_all[:, s], rt_all[:, s], bt_all[:, s], kt_all[:, s], bp_all[:, s], kp_all[:, s],
                           v_all[:, s], g_end[:, s], strict, incl))
    AT, RT, BT, KT, BP, KP, V, GE, ST, IN = zip(*chains)
    hd = range(len(chains))

    g = [_bdot_nt(jnp.concatenate([AT[i], RT[i]], axis=0), jnp.concatenate([BT[i], KT[i]], axis=0)) for i in hd]
    p = [jnp.where(ST[i], g[i][:T, :T], 0.0) for i in hd]
    u = [_bdot(jnp.where(ST[i], g[i][:T, T:], 0.0), V[i]) for i in hd]
    x = [jnp.concatenate([AT[i], u[i]], axis=1) for i in hd]
    n = 1
    while 2 * n < T:
        px = [_bdot(p[i], jnp.concatenate([x[i], p[i]], axis=1)) for i in hd]
        x = [x[i] + px[i][:, :2 * dh] for i in hd]
        p = [px[i][:, 2 * dh:] for i in hd]
        n *= 2
    x = [x[i] + _bdot(p[i], x[i]) for i in hd]
    z = [jnp.concatenate([x[i], jnp.concatenate([zeros, V[i]], axis=1)], axis=0) for i in hd]
    mx = [_bdot(jnp.concatenate([jnp.where(IN[i], g[i][T:, :T], 0.0), jnp.where(IN[i], g[i][T:, T:], 0.0)], axis=1),
                z[i]) for i in hd]
    bx = [_bdot_tn(jnp.concatenate([BP[i], KP[i]], axis=0), z[i]) for i in hd]
    yh = [_bdot(jnp.concatenate([RT[i] + mx[i][:, :dh], jnp.where(eye, GE[i], 0.0) + bx[i][:, :dh]], axis=0),
                h_ref[i]) for i in hd]
    for i in hd:
        d, h = divmod(i, n_heads)
        y_refs[d][0, :, h * dh:(h + 1) * dh] = yh[i][:T] + mx[i][:, dh:]
        h_ref[i] = yh[i][T:] + bx[i][:, dh:]


def _back_chunk(c, n_ctx, nc):
    return jnp.where(c < n_ctx, n_ctx - 1 - c, nc - 1 - c + n_ctx)


def _rwkv_scan(fwd, bwd, n_ctx):
    B, L, W = fwd[0].shape
    n_heads = W // HEAD_DIM
    nc = L // RW_CHUNK
    blk = (1, RW_CHUNK, W)
    f_spec = pl.BlockSpec(blk, lambda b, c: (b, c, 0))
    b_spec = pl.BlockSpec(blk, lambda b, c: (b, _back_chunk(c, n_ctx, nc), 0))
    return pl.pallas_call(
        functools.partial(_rwkv_kernel, n_heads=n_heads),
        grid=(B, nc),
        in_specs=[f_spec] * 6 + [b_spec] * 6,
        out_specs=[f_spec, b_spec],
        out_shape=[jax.ShapeDtypeStruct((B, L, W), f32)] * 2,
        scratch_shapes=[pltpu.VMEM((2 * n_heads, HEAD_DIM, HEAD_DIM), f32)],
        compiler_params=pltpu.CompilerParams(dimension_semantics=("parallel", "arbitrary"),
                                             vmem_limit_bytes=VMEM_LIMIT),
        name="rwkv")(*fwd, *bwd)


def _mlstm_kernel(*refs, n_heads, dirs):
    T = ML_CHUNK
    dh = HEAD_DIM
    nd_ = len(dirs)
    h_refs, c_ref, m_ref = refs[5 * nd_:6 * nd_], refs[6 * nd_], refs[6 * nd_ + 1]

    @pl.when(pl.program_id(1) == 0)
    def _():
        c_ref[...] = jnp.zeros_like(c_ref)
        m_ref[...] = jnp.zeros_like(m_ref)

    row = lax.broadcasted_iota(jnp.int32, (T, T), 0)
    col = lax.broadcasted_iota(jnp.int32, (T, T), 1)
    one_col = jnp.ones((T, dh), f32)
    chains = []
    for j, d in enumerate(dirs):
        q_ref, k_ref, v_ref, gc_ref, gr_ref = refs[5 * j:5 * j + 5]
        lower = row >= col if d == 0 else row <= col
        tri_l = jnp.where(lower, 1.0, 0.0).astype(bf16)
        tri_u = jnp.where(row <= col if d == 0 else row >= col, 1.0, 0.0).astype(bf16)
        gc = gc_ref[0]
        gr = gr_ref[0, 0]
        li_c = gc[:, :n_heads]
        b_c = _split_dot(tri_l, gc[:, n_heads:])
        li_r = gr[:n_heads, :]
        b_r = _split_dot_r(gr[n_heads:, :], tri_u)
        b_end = b_c[T - 1:T, :] if d == 0 else b_c[0:1, :]
        m_row = m_ref[j:j + 1, 0:n_heads]
        gcol = b_c + m_row
        run = _running_max(li_c - b_c, d)
        m_t = jnp.maximum(gcol, b_c + run)
        run_end = run[T - 1:T, :] if d == 0 else run[0:1, :]
        m_new = jnp.maximum(b_end + m_row, b_end + run_end)
        inter = jnp.exp(gcol - m_t)
        e_neg = jnp.exp(-m_t)
        bm = b_c - m_t
        carry = jnp.exp(b_end + m_row - m_new)
        kws = jnp.exp(b_end - b_c + li_c - m_new)
        m_ref[j:j + 1, 0:n_heads] = m_new
        q_all, k_all, v_all = q_ref[0], k_ref[0], v_ref[0]
        for h in range(n_heads):
            s = slice(h * dh, (h + 1) * dh)
            c1 = slice(h, h + 1)
            chains.append((q_all[:, s], k_all[:, s], jnp.concatenate([v_all[:, s], one_col], axis=1),
                           bm[:, c1], inter[:, c1], e_neg[:, c1], kws[:, c1], carry[:, c1],
                           b_r[c1, :], li_r[c1, :], lower))
    Q, K, VA, BM, INT, ENEG, KWS, CARRY, BR, LIR, LO = zip(*chains)
    hd = range(len(chains))
    state = [c_ref[i] for i in hd]

    dmat = [jnp.exp(jnp.where(LO[i], BM[i] - BR[i] + LIR[i], NEG_BIG)) for i in hd]
    qk = [_bdot_nt(Q[i], K[i]) for i in hd]
    nd = [_bdot(jnp.concatenate([INT[i] * Q[i], qk[i] * dmat[i]], axis=1),
                jnp.concatenate([state[i], VA[i]], axis=0)) for i in hd]
    upd = [_bdot_tn(K[i] * KWS[i], VA[i]) for i in hd]
    for i in hd:
        j, h = divmod(i, n_heads)
        h_refs[j][0, :, h * dh:(h + 1) * dh] = nd[i][:, :dh] / jnp.maximum(jnp.abs(nd[i][:, dh:]), ENEG[i])
        c_ref[i] = CARRY[i] * state[i] + upd[i]


def _mlstm_scan(q, k, v, li, lf, n_ctx):
    B, L, W = k.shape
    H = W // HEAD_DIM
    nc = L // ML_CHUNK
    gcol = jnp.concatenate([li, lf], axis=-1)
    grow = jnp.swapaxes(gcol.reshape(2 * B, nc, ML_CHUNK, 2 * H), 2, 3)

    def specs(d):
        def chunk(c):
            return c if d == 0 else _back_chunk(c, n_ctx, nc)
        seq = pl.BlockSpec((1, ML_CHUNK, W), lambda b, c: (b, chunk(c), 0))
        return seq, [seq] * 3 + [pl.BlockSpec((1, ML_CHUNK, 2 * H), lambda b, c: (b + d * B, chunk(c), 0)),
                                 pl.BlockSpec((1, 1, 2 * H, ML_CHUNK), lambda b, c: (b + d * B, chunk(c), 0, 0))]

    outs = []
    for d in (0, 1):
        out_spec, in_specs = specs(d)
        outs.append(pl.pallas_call(
            functools.partial(_mlstm_kernel, n_heads=H, dirs=(d,)),
            grid=(B, nc),
            in_specs=in_specs,
            out_specs=[out_spec],
            out_shape=[jax.ShapeDtypeStruct((B, L, W), f32)],
            scratch_shapes=[pltpu.VMEM((H, HEAD_DIM, 2 * HEAD_DIM), f32),
                            pltpu.VMEM((H, LANE), f32)],
            compiler_params=pltpu.CompilerParams(dimension_semantics=("parallel", "arbitrary"),
                                                 vmem_limit_bytes=VMEM_LIMIT),
            name="mlstm")(q, k, v, gcol, grow)[0])
    return outs


def _na_row_start(r, n_rows, win_r):
    return jnp.clip(r - win_r // 2, 0, n_rows - win_r)


def _na_kernel(qr_ref, qp_ref, k_ref, v_ref, kc_ref, vc_ref, bias_ref, o_ref, *, n_heads, n_rows, win_r):
    dh = HEAD_DIM
    r = pl.program_id(1)
    start = pl.multiple_of(_na_row_start(r, n_rows, win_r) * GRID_W, GRID_W)
    kwin = k_ref[0, pl.ds(start, win_r * GRID_W), :]
    vwin = v_ref[0, pl.ds(start, win_r * GRID_W), :]
    qr = qr_ref[0]
    qp = qp_ref[0]
    kc = kc_ref[0]
    vc = vc_ref[0]
    hd = range(n_heads)
    sls = [slice(h * dh, (h + 1) * dh) for h in hd]
    s_loc = [_bdot_nt(qr[:, sls[h]], kwin[:, sls[h]]) + bias_ref[h, 0] for h in hd]
    s_ctx = [_bdot_nt(qp[:, s], kc[:, s]) for s in sls]
    m = [jnp.maximum(jnp.max(s_loc[h], axis=1, keepdims=True), jnp.max(s_ctx[h], axis=1, keepdims=True)) for h in hd]
    p = [jnp.concatenate([jnp.exp(s_loc[h] - m[h]), jnp.exp(s_ctx[h] - m[h])], axis=1) for h in hd]
    o = [_bdot(p[h], jnp.concatenate([vwin[:, sls[h]], vc[:, sls[h]]], axis=0)) for h in hd]
    o_ref[0] = jnp.concatenate([o[h] / jnp.sum(p[h], axis=1, keepdims=True) for h in hd],
                               axis=1).astype(o_ref.dtype)


def _na_attention(q_rot, q_plain, k_rot, v, kc, vc, bias_tab, n_rows, win_r):
    B, S, W = q_rot.shape
    Lc = kc.shape[1]
    H = W // HEAD_DIM
    n_var = bias_tab.shape[1]
    qspec = pl.BlockSpec((1, GRID_W, W), lambda b, r: (b, r, 0))
    full = pl.BlockSpec((1, S, W), lambda b, r: (b, 0, 0))
    cspec = pl.BlockSpec((1, Lc, W), lambda b, r: (b, 0, 0))

    def bias_map(b, r):
        return (0, _na_row_start(r, n_rows, win_r) - r + win_r - 1, 0, 0)

    return pl.pallas_call(
        functools.partial(_na_kernel, n_heads=H, n_rows=n_rows, win_r=win_r),
        grid=(B, n_rows),
        in_specs=[qspec, qspec, full, full, cspec, cspec,
                  pl.BlockSpec((H, 1, GRID_W, win_r * GRID_W), bias_map)],
        out_specs=qspec,
        out_shape=jax.ShapeDtypeStruct((B, S, W), bf16),
        compiler_params=pltpu.CompilerParams(dimension_semantics=("parallel", "arbitrary"),
                                             vmem_limit_bytes=VMEM_LIMIT),
        name="na")(q_rot, q_plain, k_rot, v, kc, vc, bias_tab)


def _ctx_attn_kernel(q_ref, k_ref, v_ref, o_ref, *, n_heads):
    dh = HEAD_DIM
    q = q_ref[0]
    k = k_ref[0]
    v = v_ref[0]
    for h in range(n_heads):
        sl = slice(h * dh, (h + 1) * dh)
        s = _bdot_nt(q[:, sl], k[:, sl])
        m = jnp.max(s, axis=1, keepdims=True)
        p = jnp.exp(s - m)
        o_ref[0, :, sl] = _bdot(p, v[:, sl]) / jnp.sum(p, axis=1, keepdims=True)


def _ctx_attention(q, k, v):
    B, Lc, W = q.shape
    spec = pl.BlockSpec((1, Lc, W), lambda b: (b, 0, 0))
    return pl.pallas_call(
        functools.partial(_ctx_attn_kernel, n_heads=W // HEAD_DIM),
        grid=(B,), in_specs=[spec] * 3, out_specs=spec,
        out_shape=jax.ShapeDtypeStruct((B, Lc, W), f32),
        compiler_params=pltpu.CompilerParams(dimension_semantics=("parallel",), vmem_limit_bytes=VMEM_LIMIT),
        name="ctx_attn")(q, k, v)


def _moe_kernel(be_ref, nused_ref, x_ref, wgu_ref, bgu_ref, wdn_ref, bdn_ref, prev_ref, o_ref,
                wgu_scr, wdn_scr, *, d_expert):
    del prev_ref
    i = pl.program_id(0)

    @pl.when((i == 0) | (be_ref[i] != be_ref[jnp.maximum(i - 1, 0)]))
    def _():
        wgu_scr[...] = wgu_ref[0].astype(bf16)
        wdn_scr[...] = wdn_ref[0].astype(bf16)

    @pl.when(i < nused_ref[0])
    def _():
        gu = jnp.dot(x_ref[...], wgu_scr[...], preferred_element_type=f32) + bgu_ref[0]
        gt = jnp.minimum(gu[:, :d_expert], SWIGLU_LIMIT)
        up = jnp.clip(gu[:, d_expert:], -SWIGLU_LIMIT, SWIGLU_LIMIT)
        act = (up + 1.0) * (gt / (1.0 + jnp.exp(-SWIGLU_ALPHA * gt)))
        y = jnp.dot(act.astype(bf16), wdn_scr[...], preferred_element_type=f32) + bdn_ref[0]
        o_ref[...] = y.astype(o_ref.dtype)

    @pl.when(i >= nused_ref[0])
    def _():
        o_ref[...] = jnp.zeros_like(o_ref)


def _moe_experts(xg, block_e, n_used, w_gu, b_gu, w_dn, b_dn, layer, ys, blk_off):
    R, D = xg.shape
    n_blocks = R // MOE_BLOCK
    F2 = w_gu.shape[-1]
    grid_spec = pltpu.PrefetchScalarGridSpec(
        num_scalar_prefetch=2,
        grid=(n_blocks,),
        in_specs=[pl.BlockSpec((MOE_BLOCK, D), lambda i, be, nu: (i, 0)),
                  pl.BlockSpec((None, 1, D, F2), lambda i, be, nu: (layer, be[i], 0, 0)),
                  pl.BlockSpec((None, 1, 1, F2), lambda i, be, nu: (layer, be[i], 0, 0)),
                  pl.BlockSpec((None, 1, F2 // 2, D), lambda i, be, nu: (layer, be[i], 0, 0)),
                  pl.BlockSpec((None, 1, 1, D), lambda i, be, nu: (layer, be[i], 0, 0)),
                  pl.BlockSpec(memory_space=pl.ANY)],
        out_specs=pl.BlockSpec((MOE_BLOCK, D), lambda i, be, nu: (i + blk_off, 0)),
        scratch_shapes=[pltpu.VMEM((D, F2), bf16), pltpu.VMEM((F2 // 2, D), bf16)])
    return pl.pallas_call(
        functools.partial(_moe_kernel, d_expert=F2 // 2),
        grid_spec=grid_spec,
        out_shape=jax.ShapeDtypeStruct(ys.shape, ys.dtype),
        input_output_aliases={7: 0},
        compiler_params=pltpu.CompilerParams(dimension_semantics=("arbitrary",), vmem_limit_bytes=VMEM_LIMIT),
        name="moe")(block_e, n_used, xg, w_gu, b_gu[:, :, None, :], w_dn, b_dn[:, :, None, :], ys)


def _moe(h_hi, h_lo, w_router, b_router, w_gu, b_gu, w_dn, b_dn, layer):
    T, D = h_hi.shape
    E = w_router.shape[1]
    A = T * TOP_K
    w_hi = w_router.astype(bf16)
    w_lo = (w_router - w_hi.astype(f32)).astype(bf16)
    logits = _mm(h_hi, w_hi) + _mm(h_lo, w_hi) + _mm(h_hi, w_lo) + b_router
    top_val, top_idx = lax.top_k(logits, TOP_K)
    gate = jax.nn.softmax(top_val, axis=-1)
    flat_e = top_idx.reshape(-1).astype(jnp.int32)
    onehot = flat_e[:, None] == jnp.arange(E, dtype=jnp.int32)
    counts = jnp.sum(onehot.astype(jnp.int32), axis=0)
    padded = (counts + MOE_BLOCK - 1) // MOE_BLOCK * MOE_BLOCK
    pad_end = jnp.cumsum(padded)
    pad_start = pad_end - padded
    start = jnp.cumsum(counts) - counts
    order = jnp.argsort(flat_e).astype(jnp.int32)
    inv = jnp.argsort(order).astype(jnp.int32)
    shift_e = pad_start - start
    pos = (inv + jnp.sum(jnp.where(onehot, shift_e, 0), axis=1)).reshape(T, TOP_K)
    n_blocks = (A + MOE_BLOCK - 1) // MOE_BLOCK + E
    blk_first = jnp.arange(n_blocks, dtype=jnp.int32) * MOE_BLOCK
    block_e = jnp.minimum(jnp.sum((pad_end[None, :] <= blk_first[:, None]).astype(jnp.int32), axis=1), E - 1)
    n_used = (pad_end[-1:] // MOE_BLOCK).astype(jnp.int32)
    slot = jnp.arange(n_blocks * MOE_BLOCK, dtype=jnp.int32).reshape(n_blocks, MOE_BLOCK)
    off_in = slot - pad_start[block_e][:, None]
    valid = off_in < counts[block_e][:, None]
    srt = jnp.clip(start[block_e][:, None] + off_in, 0, A - 1)
    tok_buf = jnp.where(valid, order[srt.reshape(-1)].reshape(srt.shape) // TOP_K, 0).reshape(-1)
    ys = jnp.zeros((n_blocks * MOE_BLOCK, D), bf16)
    per = -(-n_blocks // MOE_SPLIT)
    for q in range(MOE_SPLIT):
        lo, hi = q * per, min((q + 1) * per, n_blocks)
        if lo >= hi:
            break
        xg = h_hi[tok_buf[lo * MOE_BLOCK:hi * MOE_BLOCK]]
        ys = _moe_experts(xg, block_e[lo:hi], n_used - lo, w_gu, b_gu, w_dn, b_dn, layer, ys, lo)
    out = gate[:, 0:1] * ys[pos[:, 0]]
    for j in range(1, TOP_K):
        out = out + gate[:, j:j + 1] * ys[pos[:, j]]
    return out


def _rms_norm(x, g):
    return x * lax.rsqrt(jnp.mean(x * x, -1, keepdims=True) + NORM_EPS) * g


def _group_norm(y, g, b, n_heads, eps):
    shp = y.shape
    yh = y.reshape(shp[:-1] + (n_heads, shp[-1] // n_heads))
    mu = jnp.mean(yh, -1, keepdims=True)
    var = jnp.mean(jnp.square(yh - mu), -1, keepdims=True)
    out = ((yh - mu) * lax.rsqrt(var + eps)).reshape(shp) * g
    return out if b is None else out + b


def _dft_mats(n, scale):
    j = jnp.arange(n, dtype=jnp.int32)
    m = (j[:, None] * j[None, :]) % n
    ang = m.astype(f32) * (2.0 * math.pi / n)
    return jnp.cos(ang) * scale, jnp.sin(ang) * scale


def _axial_rope(t, rows, cols):
    half = HEAD_DIM // 2
    nf = half // 2
    inv = ROPE_THETA ** (-jnp.arange(nf, dtype=f32) / nf)

    def rot(u, pos):
        ang = pos.astype(f32)[:, None] * inv
        cos = jnp.cos(ang)[None, :, None, :]
        sin = jnp.sin(ang)[None, :, None, :]
        u1, u2 = u[..., :nf], u[..., nf:]
        return jnp.concatenate([u1 * cos - u2 * sin, u1 * sin + u2 * cos], -1)

    return jnp.concatenate([rot(t[..., :half], rows), rot(t[..., half:], cols)], -1)


def _seg_map(fn, u, Lc):
    return jnp.concatenate([fn(u[:, :Lc]), fn(u[:, Lc:])], axis=1)


def _shift(u, direction):
    if direction == 0:
        return jnp.pad(u[:, :-1], ((0, 0), (1, 0), (0, 0)))
    return jnp.pad(u[:, 1:], ((0, 0), (0, 1), (0, 0)))


def _conv3(u, w):
    return w[0] * _shift(u, 0) + w[1] * u + w[2] * _shift(u, 1)


def _na_bias_table(rpb, win_r):
    H = rpb.shape[0]
    w = jnp.arange(GRID_W)
    col_start = jnp.clip(w - NA_WIN_COLS // 2, 0, GRID_W - NA_WIN_COLS)
    col_in = (w[None, :] >= col_start[:, None]) & (w[None, :] < col_start[:, None] + NA_WIN_COLS)
    col_off = jnp.clip(w[None, :] - w[:, None], 1 - NA_WIN_COLS, NA_WIN_COLS - 1) + NA_WIN_COLS - 1
    per_row = jnp.where(col_in[None, None], rpb[:, :, col_off], NEG_BIG)
    tabs = []
    for j in range(win_r):
        d0 = j - (win_r - 1)
        rows = [per_row[:, d0 + a + NA_WIN_ROWS - 1] for a in range(win_r)]
        tabs.append(jnp.concatenate(rows, axis=-1))
    return jnp.stack(tabs, axis=1)


def _mixer(hcat, lp, B, Lc, S, last):
    D = hcat.shape[-1]
    BW = D // N_BRANCHES
    H = BW // HEAD_DIM
    Lt = Lc + S
    M = B * Lt
    w_in = lp['w_in']
    names = ('na_k', 'na_v', 'rw_k', 'rw_v', 'rw_wl', 'rw_al', 'ml_k', 'ml_v', 'ml_g',
             'fo', 'na_q', 'rw_r', 'rw_gl', 'ml_q', 'ml_o', 'gates')
    sizes = (BW, BW, BW, BW, 2 * RW_DECAY_LORA, 2 * RW_ICL_LORA, BW, BW, 4 * H,
             BW, BW, BW, RW_GATE_LORA, BW, BW, N_BRANCHES * D)
    src, o = {}, 0
    for nm, sz in zip(names, sizes):
        src[nm] = (o, sz)
        o += sz
    groups = (('na_k', 'na_v', 'na_q'), ('rw_k', 'rw_v', 'rw_r', 'rw_wl', 'rw_al', 'rw_gl'),
              ('ml_k', 'ml_v', 'ml_q', 'ml_o', 'ml_g'), ('fo',))
    cols, dst, o = [], {}, 0
    for grp in groups:
        for nm in grp:
            a, sz = src[nm]
            cols.append(w_in[:, a:a + sz])
            dst[nm] = (o, sz)
            o += sz
        pad = (-o) % BW
        if pad:
            cols.append(jnp.zeros((D, pad), w_in.dtype))
            o += pad
    gate_off = o
    a, sz = src['gates']
    w2 = jnp.concatenate(cols + [w_in[:, a:a + sz]], axis=1)
    h2d = hcat.reshape(M, D).astype(bf16)
    px = _mm_w(h2d, w2, 0, gate_off, out_dtype=bf16).reshape(B, Lt, gate_off)

    def part(nm):
        a_, sz_ = dst[nm]
        return px[..., a_:a_ + sz_].astype(f32)

    na_k, na_v, na_q = part('na_k'), part('na_v'), part('na_q')
    rw_k, rw_v, rw_r = part('rw_k'), part('rw_v'), part('rw_r')
    rw_wl, rw_al, rw_gl = part('rw_wl'), part('rw_al'), part('rw_gl')
    ml_k, ml_v, ml_q, ml_o, ml_g = part('ml_k'), part('ml_v'), part('ml_q'), part('ml_o'), part('ml_g')
    fo = part('fo')

    C = BW // FOURIER_GROUPS
    cc, sc = _dft_mats(C, C ** -0.5)
    eye_g = jnp.eye(FOURIER_GROUPS, dtype=f32)
    chan = jnp.concatenate([jnp.kron(eye_g, cc), -jnp.kron(eye_g, sc)], axis=1)
    uc = _mm(fo.reshape(M, BW), chan).reshape(B, Lt, 2 * BW)

    def pos_dft(u, L):
        cl, sl_ = _dft_mats(L, L ** -0.5)
        rhs = jnp.concatenate([jnp.swapaxes(u[..., :BW], 0, 1).reshape(L, B * BW),
                               jnp.swapaxes(u[..., BW:], 0, 1).reshape(L, B * BW)], axis=0)
        y = _mm(jnp.concatenate([cl, sl_], axis=1), rhs, out_dtype=bf16)
        return jnp.swapaxes(y.reshape(L, B, BW), 0, 1)

    y_four = jnp.concatenate([pos_dft(uc[:, :Lc], Lc), pos_dft(uc[:, Lc:], S)], axis=1)

    n_rows = S // GRID_W
    win_r = min(NA_WIN_ROWS, n_rows)
    scale = HEAD_DIM ** -0.5
    t = jnp.arange(S)
    q_lat = na_q[:, Lc:].reshape(B, S, H, HEAD_DIM)
    k_lat = na_k[:, Lc:].reshape(B, S, H, HEAD_DIM)
    q_rot = (_axial_rope(q_lat, t // GRID_W, t % GRID_W) * scale).reshape(B, S, BW).astype(bf16)
    k_rot = _axial_rope(k_lat, t // GRID_W, t % GRID_W).reshape(B, S, BW).astype(bf16)
    q_plain = (na_q[:, Lc:] * scale).astype(bf16)
    kc = na_k[:, :Lc].astype(bf16)
    vc = na_v[:, :Lc].astype(bf16)
    bias_tab = _na_bias_table(lp['na_rpb'], win_r)
    y_na_x = _na_attention(q_rot, q_plain, k_rot, na_v[:, Lc:].astype(bf16), kc, vc, bias_tab, n_rows, win_r)
    if last:
        y_na_c = jnp.zeros((B, Lc, BW), bf16)
    else:
        y_na_c = _ctx_attention((na_q[:, :Lc] * scale).astype(bf16), kc, vc).astype(bf16)
    y_na = jnp.concatenate([y_na_c, y_na_x], axis=1)

    def flip_seg(u):
        return _seg_map(lambda s: jnp.flip(s, axis=1), u, Lc)

    seqs = []
    bonus_in = []
    for d in (0, 1):
        mu = lp['rw_mu_rkv'][d]
        mu_wa = lp['rw_mu_wa'][d]

        def lerp(u, m_):
            return u + m_ * (_seg_map(lambda s: _shift(s, d), u, Lc) - u)

        r_l = lerp(rw_r, mu[:BW])
        k_l = lerp(rw_k, mu[BW:2 * BW])
        v_l = lerp(rw_v, mu[2 * BW:])
        wl_l = lerp(rw_wl[..., d * RW_DECAY_LORA:(d + 1) * RW_DECAY_LORA], mu_wa[:RW_DECAY_LORA])
        al_l = lerp(rw_al[..., d * RW_ICL_LORA:(d + 1) * RW_ICL_LORA], mu_wa[RW_DECAY_LORA:])
        w_pre = lp['rw_w0'][d] + _mm_w(jnp.tanh(wl_l).reshape(M, -1), lp['rw_w_up'][d]).reshape(B, Lt, BW)
        w_log = -jax.nn.softplus(-w_pre) - 0.5
        lw = -jnp.exp(w_log)
        a_icl = jax.nn.sigmoid(lp['rw_a0'][d] + _mm_w(al_l.reshape(M, -1), lp['rw_a_up'][d]).reshape(B, Lt, BW))
        kk = (k_l * lp['rw_k_k']).reshape(B, Lt, H, HEAD_DIM)
        kk = (kk / jnp.maximum(jnp.sqrt(jnp.sum(kk * kk, -1, keepdims=True)), 1e-12)).reshape(B, Lt, BW)
        k_mod = k_l * (1.0 + (a_icl - 1.0) * lp['rw_k_a'])
        seqs.append((lw, k_mod, v_l, -kk, kk * a_icl, r_l))
        bonus_in.append((r_l, k_mod, v_l))
    ys = _rwkv_scan(seqs[0], seqs[1], n_ctx=Lc // RW_CHUNK)
    y_rw = 0.0
    for d in (0, 1):
        y_d = _group_norm(ys[d], lp['rw_gn_g'][d], lp['rw_gn_b'][d], H, RW_GN_EPS)
        r_l, k_mod, v_l = bonus_in[d]
        hs = (B, Lt, H, HEAD_DIM)
        bonus = jnp.sum(r_l.reshape(hs) * k_mod.reshape(hs) * lp['rw_r_k'][d], -1, keepdims=True) * v_l.reshape(hs)
        y_rw = y_rw + y_d + bonus.reshape(B, Lt, BW)
    y_rw = y_rw * _mm_w(jax.nn.sigmoid(rw_gl).reshape(M, -1), lp['rw_g_up']).reshape(B, Lt, BW)

    k_m = jax.nn.silu(_seg_map(lambda s: _conv3(s, lp['ml_conv_k']), ml_k, Lc)) * scale
    q_m = jax.nn.silu(_seg_map(lambda s: _conv3(s, lp['ml_conv_q']), ml_q, Lc))
    g4 = ml_g.reshape(B, Lt, 4, H) + lp['ml_gate_b']
    li = g4[:, :, :2]
    lf = jax.nn.log_sigmoid(g4[:, :, 2:])
    li_s = jnp.concatenate([li[:, :, 0], li[:, :, 1]], axis=0)
    lf_s = jnp.concatenate([lf[:, :, 0], lf[:, :, 1]], axis=0)
    hs_ = _mlstm_scan(q_m, k_m, ml_v, li_s, lf_s, n_ctx=Lc // ML_CHUNK)
    h_sum = hs_[0] + hs_[1]
    y_ml = _group_norm(h_sum, lp['ml_gn_g'], None, H, GN_EPS) * jax.nn.sigmoid(ml_o)

    ys4 = [u.reshape(M, BW).astype(bf16) for u in (y_four, y_na, y_rw, y_ml)]
    acc = _merge(h2d, ys4, w2, gate_off, lp['w_branch'], lp['layer'])
    return _mm_w(acc, lp['w_out'], lead=(lp['layer'],)).reshape(B, Lt, D)


def kernel(x, c, ctx, c_ctx, w_ada, b_ada, g_norm1, g_norm2, w_in, na_rpb, rw_mu_rkv, rw_mu_wa, rw_w_up, rw_w0, rw_a_up, rw_a0, rw_g_up, rw_k_k, rw_k_a, rw_r_k, rw_gn_g, rw_gn_b, ml_conv_q, ml_conv_k, ml_gate_b, ml_gn_g, w_branch, w_out, w_router, b_router, w_gu, b_gu, w_dn, b_dn, g_final):
    B, S, D = x.shape
    Lc = ctx.shape[1]
    depth = w_ada.shape[0]
    mod_in = jnp.concatenate([jax.nn.silu(c), jax.nn.silu(c_ctx)[None]], axis=0)
    mod_in = jnp.pad(mod_in, ((0, (-(B + 1)) % 16), (0, 0)))
    Lt = Lc + S
    pending = None
    for l in range(depth):
        last = l == depth - 1
        lp = {'w_in': w_in[l], 'na_rpb': na_rpb[l], 'rw_mu_rkv': rw_mu_rkv[l], 'rw_mu_wa': rw_mu_wa[l],
              'rw_w_up': rw_w_up[l], 'rw_w0': rw_w0[l], 'rw_a_up': rw_a_up[l], 'rw_a0': rw_a0[l],
              'rw_g_up': rw_g_up[l], 'rw_k_k': rw_k_k[l], 'rw_k_a': rw_k_a[l], 'rw_r_k': rw_r_k[l],
              'rw_gn_g': rw_gn_g[l], 'rw_gn_b': rw_gn_b[l], 'ml_conv_q': ml_conv_q[l],
              'ml_conv_k': ml_conv_k[l], 'ml_gate_b': ml_gate_b[l], 'ml_gn_g': ml_gn_g[l],
              'w_branch': w_branch, 'w_out': w_out, 'layer': l}
        mod = _mm_w(mod_in, w_ada, lead=(l,)) + b_ada[l]
        mx = jnp.split(mod[:B], 6, axis=-1)
        mc = jnp.split(mod[B], 6, axis=-1)
        if pending is None:
            hx, _ = _norm_mod(x, g_norm1[l], mx[1], mx[0])
        else:
            x, hx, _ = _norm_mod(x, g_norm1[l], mx[1], mx[0], res=pending)
        hc = _rms_norm(ctx, g_norm1[l]) * (1.0 + mc[1]) + mc[0]
        y = _mixer(jnp.concatenate([hc.astype(bf16), hx], axis=1), lp, B, Lc, S, last)
        res = (y.reshape(B * Lt, D), Lc, Lt, mx[2])
        if last:
            x, hx2_hi, hx2_lo = _norm_mod(x, g_norm2[l], mx[4], mx[3], res=res)
            ym = _moe(hx2_hi.reshape(-1, D), hx2_lo.reshape(-1, D), w_router[l], b_router[l],
                      w_gu, b_gu, w_dn, b_dn, l)
        else:
            x, hx2 = _norm_mod(x, g_norm2[l], mx[4], mx[3], res=res, split=False)
            ctx = ctx + mc[2] * y[:, :Lc]
            hc2 = _rms_norm(ctx, g_norm2[l]) * (1.0 + mc[4]) + mc[3]
            h_all = jnp.concatenate([hx2.reshape(-1, D), hc2.reshape(-1, D)], axis=0)
            h_hi = h_all.astype(bf16)
            n_x = B * S
            ym = _moe(h_hi, (h_all - h_hi.astype(f32)).astype(bf16),
                      w_router[l], b_router[l], w_gu, b_gu, w_dn, b_dn, l)
            ctx = ctx + mc[5] * ym[n_x:].reshape(ctx.shape)
        pending = (ym, 0, S, mx[5])
    zero = jnp.zeros((B, D), f32)
    return _norm_mod(x, g_final, zero, zero, res=pending, emit_x=False, split=False)[0]
```

```python
import functools
import math

import jax
import jax.numpy as jnp
from jax import lax
from jax.experimental import pallas as pl
from jax.experimental.pallas import tpu as pltpu

f32 = jnp.float32
bf16 = jnp.bfloat16

HEAD_DIM = 64
GRID_W = 64
N_BRANCHES = 4
FOURIER_GROUPS = 4
NA_WIN_ROWS = 8
NA_WIN_COLS = 16
ROPE_THETA = 10000.0
RW_DECAY_LORA = 64
RW_ICL_LORA = 64
RW_GATE_LORA = 128
RW_GN_EPS = 64e-5
ML_CHUNK = 64
RW_CHUNK = 64
TOP_K = 4
SWIGLU_LIMIT = 7.0
SWIGLU_ALPHA = 1.702
MOE_BLOCK = 256
MOE_SPLIT = 4
NORM_EPS = 1e-6
GN_EPS = 1e-5
NEG_BIG = -1e30

LANE = 128
VMEM_LIMIT = 56 * 1024 * 1024


def _bdot(a, b):
    return jnp.dot(a.astype(bf16), b.astype(bf16), preferred_element_type=f32)


def _bdot_nt(a, b):
    return lax.dot_general(a.astype(bf16), b.astype(bf16), (((1,), (1,)), ((), ())),
                           preferred_element_type=f32)


def _bdot_tn(a, b):
    return lax.dot_general(a.astype(bf16), b.astype(bf16), (((0,), (0,)), ((), ())),
                           preferred_element_type=f32)


def _split_dot(tri, x):
    hi = x.astype(bf16)
    lo = (x - hi.astype(f32)).astype(bf16)
    return (jnp.dot(tri, hi, preferred_element_type=f32) + jnp.dot(tri, lo, preferred_element_type=f32))


def _split_dot_r(x, tri):
    hi = x.astype(bf16)
    lo = (x - hi.astype(f32)).astype(bf16)
    return (jnp.dot(hi, tri, preferred_element_type=f32) + jnp.dot(lo, tri, preferred_element_type=f32))


def _running_max(x, d):
    T = x.shape[0]
    t = lax.broadcasted_iota(jnp.int32, x.shape, 0)
    s = 1
    while s < T:
        if d == 0:
            x = jnp.maximum(x, jnp.where(t >= s, pltpu.roll(x, s, axis=0), NEG_BIG))
        else:
            x = jnp.maximum(x, jnp.where(t < T - s, pltpu.roll(x, T - s, axis=0), NEG_BIG))
        s *= 2
    return x


def _pick(dim, cap, align):
    if dim <= cap:
        return dim
    t = cap - cap % align
    while t >= align:
        if dim % t == 0:
            return t
        t -= align
    return dim


def _mm_kernel_single(a_ref, b_ref, o_ref):
    o_ref[...] = jnp.dot(a_ref[...], b_ref[...], preferred_element_type=f32).astype(o_ref.dtype)


def _mm_kernel_acc(a_ref, b_ref, o_ref, acc_ref, *, nk):
    k = pl.program_id(2)

    @pl.when(k == 0)
    def _():
        acc_ref[...] = jnp.zeros_like(acc_ref)

    acc_ref[...] += jnp.dot(a_ref[...], b_ref[...], preferred_element_type=f32)

    @pl.when(k == nk - 1)
    def _():
        o_ref[...] = acc_ref[...].astype(o_ref.dtype)


def _mm(a, b, out_dtype=f32):
    a = a.astype(bf16)
    b = b.astype(bf16)
    M, K = a.shape
    N = b.shape[1]
    tm = _pick(M, 1024, 16)
    tn = _pick(N, 1024, LANE)
    tk = _pick(K, 2048, LANE)
    nk = K // tk
    grid = (M // tm, N // tn, nk)
    in_specs = [pl.BlockSpec((tm, tk), lambda i, j, k: (i, k)),
                pl.BlockSpec((tk, tn), lambda i, j, k: (k, j))]
    out_spec = pl.BlockSpec((tm, tn), lambda i, j, k: (i, j))
    params = pltpu.CompilerParams(dimension_semantics=("parallel", "parallel", "arbitrary"),
                                  vmem_limit_bytes=VMEM_LIMIT)
    if nk == 1:
        return pl.pallas_call(_mm_kernel_single, grid=grid, in_specs=in_specs, out_specs=out_spec,
                              out_shape=jax.ShapeDtypeStruct((M, N), out_dtype),
                              compiler_params=params, name="mm")(a, b)
    return pl.pallas_call(functools.partial(_mm_kernel_acc, nk=nk), grid=grid, in_specs=in_specs,
                          out_specs=out_spec, out_shape=jax.ShapeDtypeStruct((M, N), out_dtype),
                          scratch_shapes=[pltpu.VMEM((tm, tn), f32)],
                          compiler_params=params, name="mm_acc")(a, b)


def _mm_ws_kernel(a_ref, b_ref, o_ref, w_ref):
    @pl.when(pl.program_id(1) == 0)
    def _():
        w_ref[...] = b_ref[...].astype(bf16)

    o_ref[...] = jnp.dot(a_ref[...], w_ref[...], preferred_element_type=f32).astype(o_ref.dtype)


def _mm_w(a, w, col_off=0, n_cols=None, out_dtype=f32, lead=()):
    a = a.astype(bf16)
    M, K = a.shape
    n_cols = w.shape[-1] - col_off if n_cols is None else n_cols
    tm = _pick(M, 1024, 16)
    tn = _pick(n_cols, 512, LANE)
    if col_off % tn:
        tn = math.gcd(tn, col_off)
    off = col_off // tn
    return pl.pallas_call(
        _mm_ws_kernel,
        grid=(n_cols // tn, M // tm),
        in_specs=[pl.BlockSpec((tm, K), lambda j, i: (i, 0)),
                  pl.BlockSpec((None,) * len(lead) + (K, tn), lambda j, i: tuple(lead) + (0, j + off))],
        out_specs=pl.BlockSpec((tm, tn), lambda j, i: (i, j)),
        out_shape=jax.ShapeDtypeStruct((M, n_cols), out_dtype),
        scratch_shapes=[pltpu.VMEM((K, tn), bf16)],
        compiler_params=pltpu.CompilerParams(dimension_semantics=("parallel", "arbitrary"),
                                             vmem_limit_bytes=VMEM_LIMIT),
        name="mm_w")(a, w)


def _norm_kernel(*refs, has_res, emit_x, split):
    x = refs[0][0]
    n_in = 1
    if has_res:
        x = x + refs[2][0] * refs[1][...]
        n_in = 3
    g, scale, shift = refs[n_in][...], refs[n_in + 1][0], refs[n_in + 2][0]
    outs = refs[n_in + 3:]
    h = x * lax.rsqrt(jnp.mean(x * x, axis=-1, keepdims=True) + NORM_EPS) * g * (1.0 + scale) + shift
    if emit_x:
        outs[0][0] = x
        outs = outs[1:]
    if split:
        hi = h.astype(bf16)
        outs[0][0] = hi
        if len(outs) > 1:
            outs[1][0] = (h - hi.astype(f32)).astype(bf16)
    else:
        outs[0][0] = h


def _norm_mod(x, g, scale, shift, res=None, emit_x=True, split=True, lo=True):
    B, S, D = x.shape
    tm = math.gcd(S, 256)
    row = pl.BlockSpec((1, tm, D), lambda b, i: (b, i, 0))
    vec = pl.BlockSpec((1, 1, D), lambda b, i: (b, 0, 0))
    args, in_specs = [x], [row]
    if res is not None:
        y2d, row0, row_stride, gate = res
        tm = math.gcd(math.gcd(tm, row_stride), row0) if row0 else math.gcd(tm, row_stride)
        row = pl.BlockSpec((1, tm, D), lambda b, i: (b, i, 0))
        args += [y2d, gate[:, None, :]]
        in_specs = [row, pl.BlockSpec((tm, D), lambda b, i: ((row0 + b * row_stride) // tm + i, 0)), vec]
    args += [g[None, :], scale[:, None, :], shift[:, None, :]]
    in_specs += [pl.BlockSpec((1, D), lambda b, i: (0, 0)), vec, vec]
    emit_x = emit_x and res is not None
    out_shape, out_specs = [], []
    if emit_x:
        out_shape.append(jax.ShapeDtypeStruct((B, S, D), f32))
        out_specs.append(row)
    for dt in (((bf16, bf16) if lo else (bf16,)) if split else (f32,)):
        out_shape.append(jax.ShapeDtypeStruct((B, S, D), dt))
        out_specs.append(row)
    return pl.pallas_call(
        functools.partial(_norm_kernel, has_res=res is not None, emit_x=emit_x, split=split),
        grid=(B, S // tm), in_specs=in_specs, out_specs=out_specs, out_shape=out_shape,
        compiler_params=pltpu.CompilerParams(dimension_semantics=("parallel", "parallel"),
                                             vmem_limit_bytes=VMEM_LIMIT),
        name="norm_mod")(*args)


def _merge_kernel(*refs, nb):
    h_ref = refs[0]
    y_refs, g_refs = refs[1:1 + nb], refs[1 + nb:1 + 2 * nb]
    wb_ref, o_ref, wg_scr, wb_scr = refs[1 + 2 * nb:]

    @pl.when(pl.program_id(1) == 0)
    def _():
        for i in range(nb):
            wg_scr[i] = g_refs[i][...].astype(bf16)
        wb_scr[...] = wb_ref[...].astype(bf16)

    h = h_ref[...]
    acc = None
    for i in range(nb):
        gate = jnp.dot(h, wg_scr[i], preferred_element_type=f32)
        p = jnp.dot(y_refs[i][...], wb_scr[i], preferred_element_type=f32)
        t = p / (1.0 + jnp.exp(-gate))
        acc = t if acc is None else acc + t
    o_ref[...] = acc.astype(o_ref.dtype)


def _merge(h, ys, w_all, gate_off, w_branch, layer):
    nb = len(ys)
    M, K = h.shape
    W = ys[0].shape[1]
    D = w_branch.shape[-1]
    tm = _pick(M, 1024, 16)
    tn = math.gcd(_pick(D, 256, LANE), gate_off)
    nj = D // tn
    off = gate_off // tn
    g_specs = [pl.BlockSpec((K, tn), functools.partial(lambda j, i, b: (0, off + b * nj + j), b=b))
               for b in range(nb)]
    return pl.pallas_call(
        functools.partial(_merge_kernel, nb=nb),
        grid=(nj, M // tm),
        in_specs=[pl.BlockSpec((tm, K), lambda j, i: (i, 0))]
                 + [pl.BlockSpec((tm, W), lambda j, i: (i, 0)) for _ in range(nb)]
                 + g_specs + [pl.BlockSpec((None, nb, W, tn), lambda j, i: (layer, 0, 0, j))],
        out_specs=pl.BlockSpec((tm, tn), lambda j, i: (i, j)),
        out_shape=jax.ShapeDtypeStruct((M, D), bf16),
        scratch_shapes=[pltpu.VMEM((nb, K, tn), bf16), pltpu.VMEM((nb, W, tn), bf16)],
        compiler_params=pltpu.CompilerParams(dimension_semantics=("parallel", "arbitrary"),
                                             vmem_limit_bytes=VMEM_LIMIT),
        name="merge")(h, *ys, *([w_all] * nb), w_branch)


def _rwkv_kernel(*refs, n_heads):
    T = RW_CHUNK
    dh = HEAD_DIM
    y_refs, h_ref = refs[12:14], refs[14]

    @pl.when(pl.program_id(1) == 0)
    def _():
        h_ref[...] = jnp.zeros_like(h_ref)

    row = lax.broadcasted_iota(jnp.int32, (T, T), 0)
    col = lax.broadcasted_iota(jnp.int32, (T, T), 1)
    eye = row == col
    zeros = jnp.zeros((T, dh), f32)
    chains = []
    for d in (0, 1):
        strict = row > col if d == 0 else row < col
        incl = strict | eye
        tri = jnp.where(incl, 1.0, 0.0).astype(bf16)
        lw, kk, v_all, aa, bb, rr = [r[0] for r in refs[6 * d:6 * d + 6]]
        c = _split_dot(tri, lw)
        c_end = c[T - 1:T, :] if d == 0 else c[0:1, :]
        e_neg = jnp.exp(-c)
        e_rem = jnp.exp(c_end - c)
        g_end = jnp.exp(c_end)
        at_all = aa * jnp.exp(c - lw)
        rt_all = rr * jnp.exp(c)
        bt_all = bb * e_neg
        kt_all = kk * e_neg
        bp_all = bb * e_rem
        kp_all = kk * e_rem
        for h in range(n_heads):
            s = slice(h * dh, (h + 1) * dh)
            chains.append((at_all[:, s], rt_all[:, s], bt_all[:, s], kt_all[:, s], bp_all[:, s], kp_all[:, s],
                           v_all[:, s], g_end[:, s], strict, incl))
    AT, RT, BT, KT, BP, KP, V, GE, ST, IN = zip(*chains)
    hd = range(len(chains))

    g = [_bdot_nt(jnp.concatenate([AT[i], RT[i]], axis=0), jnp.concatenate([BT[i], KT[i]], axis=0)) for i in hd]
    p = [jnp.where(ST[i], g[i][:T, :T], 0.0) for i in hd]
    u = [_bdot(jnp.where(ST[i], g[i][:T, T:], 0.0), V[i]) for i in hd]
    x = [jnp.concatenate([AT[i], u[i]], axis=1) for i in hd]
    n = 1
    while 2 * n < T:
        px = [_bdot(p[i], jnp.concatenate([x[i], p[i]], axis=1)) for i in hd]
        x = [x[i] + px[i][:, :2 * dh] for i in hd]
        p = [px[i][:, 2 * dh:] for i in hd]
        n *= 2
    x = [x[i] + _bdot(p[i], x[i]) for i in hd]
    z = [jnp.concatenate([x[i], jnp.concatenate([zeros, V[i]], axis=1)], axis=0) for i in hd]
    mx = [_bdot(jnp.concatenate([jnp.where(IN[i], g[i][T:, :T], 0.0), jnp.where(IN[i], g[i][T:, T:], 0.0)], axis=1),
                z[i]) for i in hd]
    bx = [_bdot_tn(jnp.concatenate([BP[i], KP[i]], axis=0), z[i]) for i in hd]
    yh = [_bdot(jnp.concatenate([RT[i] + mx[i][:, :dh], jnp.where(eye, GE[i], 0.0) + bx[i][:, :dh]], axis=0),
                h_ref[i]) for i in hd]
    for i in hd:
        d, h = divmod(i, n_heads)
        y_refs[d][0, :, h * dh:(h + 1) * dh] = yh[i][:T] + mx[i][:, dh:]
        h_ref[i] = yh[i][T:] + bx[i][:, dh:]


def _back_chunk(c, n_ctx, nc):
    return jnp.where(c < n_ctx, n_ctx - 1 - c, nc - 1 - c + n_ctx)


def _rwkv_scan(fwd, bwd, n_ctx):
    B, L, W = fwd[0].shape
    n_heads = W // HEAD_DIM
    nc = L // RW_CHUNK
    blk = (1, RW_CHUNK, W)
    f_spec = pl.BlockSpec(blk, lambda b, c: (b, c, 0))
    b_spec = pl.BlockSpec(blk, lambda b, c: (b, _back_chunk(c, n_ctx, nc), 0))
    return pl.pallas_call(
        functools.partial(_rwkv_kernel, n_heads=n_heads),
        grid=(B, nc),
        in_specs=[f_spec] * 6 + [b_spec] * 6,
        out_specs=[f_spec, b_spec],
        out_shape=[jax.ShapeDtypeStruct((B, L, W), f32)] * 2,
        scratch_shapes=[pltpu.VMEM((2 * n_heads, HEAD_DIM, HEAD_DIM), f32)],
        compiler_params=pltpu.CompilerParams(dimension_semantics=("parallel", "arbitrary"),
                                             vmem_limit_bytes=VMEM_LIMIT),
        name="rwkv")(*fwd, *bwd)


def _mlstm_kernel(*refs, n_heads, dirs):
    T = ML_CHUNK
    dh = HEAD_DIM
    nd_ = len(dirs)
    h_refs, c_ref, m_ref = refs[5 * nd_:6 * nd_], refs[6 * nd_], refs[6 * nd_ + 1]

    @pl.when(pl.program_id(1) == 0)
    def _():
        c_ref[...] = jnp.zeros_like(c_ref)
        m_ref[...] = jnp.zeros_like(m_ref)

    row = lax.broadcasted_iota(jnp.int32, (T, T), 0)
    col = lax.broadcasted_iota(jnp.int32, (T, T), 1)
    one_col = jnp.ones((T, dh), f32)
    chains = []
    for j, d in enumerate(dirs):
        q_ref, k_ref, v_ref, gc_ref, gr_ref = refs[5 * j:5 * j + 5]
        lower = row >= col if d == 0 else row <= col
        tri_l = jnp.where(lower, 1.0, 0.0).astype(bf16)
        tri_u = jnp.where(row <= col if d == 0 else row >= col, 1.0, 0.0).astype(bf16)
        gc = gc_ref[0]
        gr = gr_ref[0, 0]
        li_c = gc[:, :n_heads]
        b_c = _split_dot(tri_l, gc[:, n_heads:])
        li_r = gr[:n_heads, :]
        b_r = _split_dot_r(gr[n_heads:, :], tri_u)
        b_end = b_c[T - 1:T, :] if d == 0 else b_c[0:1, :]
        m_row = m_ref[j:j + 1, 0:n_heads]
        gcol = b_c + m_row
        run = _running_max(li_c - b_c, d)
        m_t = jnp.maximum(gcol, b_c + run)
        run_end = run[T - 1:T, :] if d == 0 else run[0:1, :]
        m_new = jnp.maximum(b_end + m_row, b_end + run_end)
        inter = jnp.exp(gcol - m_t)
        e_neg = jnp.exp(-m_t)
        bm = b_c - m_t
        carry = jnp.exp(b_end + m_row - m_new)
        kws = jnp.exp(b_end - b_c + li_c - m_new)
        m_ref[j:j + 1, 0:n_heads] = m_new
        q_all, k_all, v_all = q_ref[0], k_ref[0], v_ref[0]
        for h in range(n_heads):
            s = slice(h * dh, (h + 1) * dh)
            c1 = slice(h, h + 1)
            chains.append((q_all[:, s], k_all[:, s], jnp.concatenate([v_all[:, s], one_col], axis=1),
                           bm[:, c1], inter[:, c1], e_neg[:, c1], kws[:, c1], carry[:, c1],
                           b_r[c1, :], li_r[c1, :], lower))
    Q, K, VA, BM, INT, ENEG, KWS, CARRY, BR, LIR, LO = zip(*chains)
    hd = range(len(chains))
    state = [c_ref[i] for i in hd]

    dmat = [jnp.exp(jnp.where(LO[i], BM[i] - BR[i] + LIR[i], NEG_BIG)) for i in hd]
    qk = [_bdot_nt(Q[i], K[i]) for i in hd]
    nd = [_bdot(jnp.concatenate([INT[i] * Q[i], qk[i] * dmat[i]], axis=1),
                jnp.concatenate([state[i], VA[i]], axis=0)) for i in hd]
    upd = [_bdot_tn(K[i] * KWS[i], VA[i]) for i in hd]
    for i in hd:
        j, h = divmod(i, n_heads)
        h_refs[j][0, :, h * dh:(h + 1) * dh] = nd[i][:, :dh] / jnp.maximum(jnp.abs(nd[i][:, dh:]), ENEG[i])
        c_ref[i] = CARRY[i] * state[i] + upd[i]


def _mlstm_scan(q, k, v, li, lf, n_ctx):
    B, L, W = k.shape
    H = W // HEAD_DIM
    nc = L // ML_CHUNK
    gcol = jnp.concatenate([li, lf], axis=-1)
    grow = jnp.swapaxes(gcol.reshape(2 * B, nc, ML_CHUNK, 2 * H), 2, 3)

    def specs(d):
        def chunk(c):
            return c if d == 0 else _back_chunk(c, n_ctx, nc)
        seq = pl.BlockSpec((1, ML_CHUNK, W), lambda b, c: (b, chunk(c), 0))
        return seq, [seq] * 3 + [pl.BlockSpec((1, ML_CHUNK, 2 * H), lambda b, c: (b + d * B, chunk(c), 0)),
                                 pl.BlockSpec((1, 1, 2 * H, ML_CHUNK), lambda b, c: (b + d * B, chunk(c), 0, 0))]

    outs = []
    for d in (0, 1):
        out_spec, in_specs = specs(d)
        outs.append(pl.pallas_call(
            functools.partial(_mlstm_kernel, n_heads=H, dirs=(d,)),
            grid=(B, nc),
            in_specs=in_specs,
            out_specs=[out_spec],
            out_shape=[jax.ShapeDtypeStruct((B, L, W), f32)],
            scratch_shapes=[pltpu.VMEM((H, HEAD_DIM, 2 * HEAD_DIM), f32),
                            pltpu.VMEM((H, LANE), f32)],
            compiler_params=pltpu.CompilerParams(dimension_semantics=("parallel", "arbitrary"),
                                                 vmem_limit_bytes=VMEM_LIMIT),
            name="mlstm")(q, k, v, gcol, grow)[0])
    return outs


def _na_row_start(r, n_rows, win_r):
    return jnp.clip(r - win_r // 2, 0, n_rows - win_r)


def _na_kernel(qr_ref, qp_ref, k_ref, v_ref, kc_ref, vc_ref, bias_ref, o_ref, *, n_heads, n_rows, win_r):
    dh = HEAD_DIM
    r = pl.program_id(1)
    start = pl.multiple_of(_na_row_start(r, n_rows, win_r) * GRID_W, GRID_W)
    kwin = k_ref[0, pl.ds(start, win_r * GRID_W), :]
    vwin = v_ref[0, pl.ds(start, win_r * GRID_W), :]
    qr = qr_ref[0]
    qp = qp_ref[0]
    kc = kc_ref[0]
    vc = vc_ref[0]
    hd = range(n_heads)
    sls = [slice(h * dh, (h + 1) * dh) for h in hd]
    s_loc = [_bdot_nt(qr[:, sls[h]], kwin[:, sls[h]]) + bias_ref[h, 0] for h in hd]
    s_ctx = [_bdot_nt(qp[:, s], kc[:, s]) for s in sls]
    m = [jnp.maximum(jnp.max(s_loc[h], axis=1, keepdims=True), jnp.max(s_ctx[h], axis=1, keepdims=True)) for h in hd]
    p = [jnp.concatenate([jnp.exp(s_loc[h] - m[h]), jnp.exp(s_ctx[h] - m[h])], axis=1) for h in hd]
    o = [_bdot(p[h], jnp.concatenate([vwin[:, sls[h]], vc[:, sls[h]]], axis=0)) for h in hd]
    o_ref[0] = jnp.concatenate([o[h] / jnp.sum(p[h], axis=1, keepdims=True) for h in hd],
                               axis=1).astype(o_ref.dtype)


def _na_attention(q_rot, q_plain, k_rot, v, kc, vc, bias_tab, n_rows, win_r):
    B, S, W = q_rot.shape
    Lc = kc.shape[1]
    H = W // HEAD_DIM
    qspec = pl.BlockSpec((1, GRID_W, W), lambda b, r: (b, r, 0))
    full = pl.BlockSpec((1, S, W), lambda b, r: (b, 0, 0))
    cspec = pl.BlockSpec((1, Lc, W), lambda b, r: (b, 0, 0))

    def bias_map(b, r):
        return (0, _na_row_start(r, n_rows, win_r) - r + win_r - 1, 0, 0)

    return pl.pallas_call(
        functools.partial(_na_kernel, n_heads=H, n_rows=n_rows, win_r=win_r),
        grid=(B, n_rows),
        in_specs=[qspec, qspec, full, full, cspec, cspec,
                  pl.BlockSpec((H, 1, GRID_W, win_r * GRID_W), bias_map)],
        out_specs=qspec,
        out_shape=jax.ShapeDtypeStruct((B, S, W), bf16),
        compiler_params=pltpu.CompilerParams(dimension_semantics=("parallel", "arbitrary"),
                                             vmem_limit_bytes=VMEM_LIMIT),
        name="na")(q_rot, q_plain, k_rot, v, kc, vc, bias_tab)


def _ctx_attn_kernel(q_ref, k_ref, v_ref, o_ref, *, n_heads):
    dh = HEAD_DIM
    q = q_ref[0]
    k = k_ref[0]
    v = v_ref[0]
    for h in range(n_heads):
        sl = slice(h * dh, (h + 1) * dh)
        s = _bdot_nt(q[:, sl], k[:, sl])
        m = jnp.max(s, axis=1, keepdims=True)
        p = jnp.exp(s - m)
        o_ref[0, :, sl] = _bdot(p, v[:, sl]) / jnp.sum(p, axis=1, keepdims=True)


def _ctx_attention(q, k, v):
    B, Lc, W = q.shape
    spec = pl.BlockSpec((1, Lc, W), lambda b: (b, 0, 0))
    return pl.pallas_call(
        functools.partial(_ctx_attn_kernel, n_heads=W // HEAD_DIM),
        grid=(B,), in_specs=[spec] * 3, out_specs=spec,
        out_shape=jax.ShapeDtypeStruct((B, Lc, W), f32),
        compiler_params=pltpu.CompilerParams(dimension_semantics=("parallel",), vmem_limit_bytes=VMEM_LIMIT),
        name="ctx_attn")(q, k, v)


def _moe_kernel(be_ref, nused_ref, x_ref, wgu_ref, bgu_ref, wdn_ref, bdn_ref, prev_ref, o_ref,
                wgu_scr, wdn_scr, *, d_expert):
    del prev_ref
    i = pl.program_id(0)

    @pl.when((i == 0) | (be_ref[i] != be_ref[jnp.maximum(i - 1, 0)]))
    def _():
        wgu_scr[...] = wgu_ref[0].astype(bf16)
        wdn_scr[...] = wdn_ref[0].astype(bf16)

    @pl.when(i < nused_ref[0])
    def _():
        gu = jnp.dot(x_ref[...], wgu_scr[...], preferred_element_type=f32) + bgu_ref[0]
        gt = jnp.minimum(gu[:, :d_expert], SWIGLU_LIMIT)
        up = jnp.clip(gu[:, d_expert:], -SWIGLU_LIMIT, SWIGLU_LIMIT)
        act = (up + 1.0) * (gt / (1.0 + jnp.exp(-SWIGLU_ALPHA * gt)))
        y = jnp.dot(act.astype(bf16), wdn_scr[...], preferred_element_type=f32) + bdn_ref[0]
        o_ref[...] = y.astype(o_ref.dtype)

    @pl.when(i >= nused_ref[0])
    def _():
        o_ref[...] = jnp.zeros_like(o_ref)


def _moe_experts(xg, block_e, n_used, w_gu, b_gu, w_dn, b_dn, layer, ys, blk_off):
    R, D = xg.shape
    n_blocks = R // MOE_BLOCK
    F2 = w_gu.shape[-1]
    grid_spec = pltpu.PrefetchScalarGridSpec(
        num_scalar_prefetch=2,
        grid=(n_blocks,),
        in_specs=[pl.BlockSpec((MOE_BLOCK, D), lambda i, be, nu: (i, 0)),
                  pl.BlockSpec((None, 1, D, F2), lambda i, be, nu: (layer, be[i], 0, 0)),
                  pl.BlockSpec((None, 1, 1, F2), lambda i, be, nu: (layer, be[i], 0, 0)),
                  pl.BlockSpec((None, 1, F2 // 2, D), lambda i, be, nu: (layer, be[i], 0, 0)),
                  pl.BlockSpec((None, 1, 1, D), lambda i, be, nu: (layer, be[i], 0, 0)),
                  pl.BlockSpec(memory_space=pl.ANY)],
        out_specs=pl.BlockSpec((MOE_BLOCK, D), lambda i, be, nu: (i + blk_off, 0)),
        scratch_shapes=[pltpu.VMEM((D, F2), bf16), pltpu.VMEM((F2 // 2, D), bf16)])
    return pl.pallas_call(
        functools.partial(_moe_kernel, d_expert=F2 // 2),
        grid_spec=grid_spec,
        out_shape=jax.ShapeDtypeStruct(ys.shape, ys.dtype),
        input_output_aliases={7: 0},
        compiler_params=pltpu.CompilerParams(dimension_semantics=("arbitrary",), vmem_limit_bytes=VMEM_LIMIT),
        name="moe")(block_e, n_used, xg, w_gu, b_gu[:, :, None, :], w_dn, b_dn[:, :, None, :], ys)


def _moe(h_hi, h_lo, w_router, b_router, w_gu, b_gu, w_dn, b_dn, layer):
    T, D = h_hi.shape
    E = w_router.shape[1]
    A = T * TOP_K
    w_hi = w_router.astype(bf16)
    w_lo = (w_router - w_hi.astype(f32)).astype(bf16)
    logits = _mm(h_hi, w_hi) + _mm(h_lo, w_hi) + _mm(h_hi, w_lo) + b_router
    top_val, top_idx = lax.top_k(logits, TOP_K)
    gate = jax.nn.softmax(top_val, axis=-1)
    flat_e = top_idx.reshape(-1).astype(jnp.int32)
    onehot = flat_e[:, None] == jnp.arange(E, dtype=jnp.int32)
    counts = jnp.sum(onehot.astype(jnp.int32), axis=0)
    padded = (counts + MOE_BLOCK - 1) // MOE_BLOCK * MOE_BLOCK
    pad_end = jnp.cumsum(padded)
    pad_start = pad_end - padded
    start = jnp.cumsum(counts) - counts
    order = jnp.argsort(flat_e).astype(jnp.int32)
    inv = jnp.argsort(order).astype(jnp.int32)
    shift_e = pad_start - start
    pos = (inv + jnp.sum(jnp.where(onehot, shift_e, 0), axis=1)).reshape(T, TOP_K)
    n_blocks = (A + MOE_BLOCK - 1) // MOE_BLOCK + E
    blk_first = jnp.arange(n_blocks, dtype=jnp.int32) * MOE_BLOCK
    block_e = jnp.minimum(jnp.sum((pad_end[None, :] <= blk_first[:, None]).astype(jnp.int32), axis=1), E - 1)
    n_used = (pad_end[-1:] // MOE_BLOCK).astype(jnp.int32)
    slot = jnp.arange(n_blocks * MOE_BLOCK, dtype=jnp.int32).reshape(n_blocks, MOE_BLOCK)
    off_in = slot - pad_start[block_e][:, None]
    valid = off_in < counts[block_e][:, None]
    srt = jnp.clip(start[block_e][:, None] + off_in, 0, A - 1)
    tok_buf = jnp.where(valid, order[srt.reshape(-1)].reshape(srt.shape) // TOP_K, 0).reshape(-1)
    ys = jnp.zeros((n_blocks * MOE_BLOCK, D), bf16)
    per = -(-n_blocks // MOE_SPLIT)
    for q in range(MOE_SPLIT):
        lo, hi = q * per, min((q + 1) * per, n_blocks)
        if lo >= hi:
            break
        xg = h_hi[tok_buf[lo * MOE_BLOCK:hi * MOE_BLOCK]]
        ys = _moe_experts(xg, block_e[lo:hi], n_used - lo, w_gu, b_gu, w_dn, b_dn, layer, ys, lo)
    out = gate[:, 0:1] * ys[pos[:, 0]]
    for j in range(1, TOP_K):
        out = out + gate[:, j:j + 1] * ys[pos[:, j]]
    return out


def _rms_norm(x, g):
    return x * lax.rsqrt(jnp.mean(x * x, -1, keepdims=True) + NORM_EPS) * g


def _group_norm(y, g, b, n_heads, eps):
    shp = y.shape
    yh = y.reshape(shp[:-1] + (n_heads, shp[-1] // n_heads))
    mu = jnp.mean(yh, -1, keepdims=True)
    var = jnp.mean(jnp.square(yh - mu), -1, keepdims=True)
    out = ((yh - mu) * lax.rsqrt(var + eps)).reshape(shp) * g
    return out if b is None else out + b


def _dft_mats(n, scale):
    j = jnp.arange(n, dtype=jnp.int32)
    m = (j[:, None] * j[None, :]) % n
    ang = m.astype(f32) * (2.0 * math.pi / n)
    return jnp.cos(ang) * scale, jnp.sin(ang) * scale


def _axial_rope(t, rows, cols):
    half = HEAD_DIM // 2
    nf = half // 2
    inv = ROPE_THETA ** (-jnp.arange(nf, dtype=f32) / nf)

    def rot(u, pos):
        ang = pos.astype(f32)[:, None] * inv
        cos = jnp.cos(ang)[None, :, None, :]
        sin = jnp.sin(ang)[None, :, None, :]
        u1, u2 = u[..., :nf], u[..., nf:]
        return jnp.concatenate([u1 * cos - u2 * sin, u1 * sin + u2 * cos], -1)

    return jnp.concatenate([rot(t[..., :half], rows), rot(t[..., half:], cols)], -1)


def _seg_map(fn, u, Lc):
    return jnp.concatenate([fn(u[:, :Lc]), fn(u[:, Lc:])], axis=1)


def _shift(u, direction):
    if direction == 0:
        return jnp.pad(u[:, :-1], ((0, 0), (1, 0), (0, 0)))
    return jnp.pad(u[:, 1:], ((0, 0), (0, 1), (0, 0)))


def _conv3(u, w):
    return w[0] * _shift(u, 0) + w[1] * u + w[2] * _shift(u, 1)


def _na_bias_table(rpb, win_r):
    H = rpb.shape[0]
    w = jnp.arange(GRID_W)
    col_start = jnp.clip(w - NA_WIN_COLS // 2, 0, GRID_W - NA_WIN_COLS)
    col_in = (w[None, :] >= col_start[:, None]) & (w[None, :] < col_start[:, None] + NA_WIN_COLS)
    col_off = jnp.clip(w[None, :] - w[:, None], 1 - NA_WIN_COLS, NA_WIN_COLS - 1) + NA_WIN_COLS - 1
    per_row = jnp.where(col_in[None, None], rpb[:, :, col_off], NEG_BIG)
    tabs = []
    for j in range(win_r):
        d0 = j - (win_r - 1)
        rows = [per_row[:, d0 + a + NA_WIN_ROWS - 1] for a in range(win_r)]
        tabs.append(jnp.concatenate(rows, axis=-1))
    return jnp.stack(tabs, axis=1)


def _mixer(hcat, lp, B, Lc, S, last):
    D = hcat.shape[-1]
    BW = D // N_BRANCHES
    H = BW // HEAD_DIM
    Lt = Lc + S
    M = B * Lt
    w_in = lp['w_in']
    names = ('na_k', 'na_v', 'rw_k', 'rw_v', 'rw_wl', 'rw_al', 'ml_k', 'ml_v', 'ml_g',
             'fo', 'na_q', 'rw_r', 'rw_gl', 'ml_q', 'ml_o', 'gates')
    sizes = (BW, BW, BW, BW, 2 * RW_DECAY_LORA, 2 * RW_ICL_LORA, BW, BW, 4 * H,
             BW, BW, BW, RW_GATE_LORA, BW, BW, N_BRANCHES * D)
    src, o = {}, 0
    for nm, sz in zip(names, sizes):
        src[nm] = (o, sz)
        o += sz
    groups = (('na_k', 'na_v', 'na_q'), ('rw_k', 'rw_v', 'rw_r', 'rw_wl', 'rw_al', 'rw_gl'),
              ('ml_k', 'ml_v', 'ml_q', 'ml_o', 'ml_g'), ('fo',))
    cols, dst, o = [], {}, 0
    for grp in groups:
        for nm in grp:
            a, sz = src[nm]
            cols.append(w_in[:, a:a + sz])
            dst[nm] = (o, sz)
            o += sz
        pad = (-o) % BW
        if pad:
            cols.append(jnp.zeros((D, pad), w_in.dtype))
            o += pad
    gate_off = o
    a, sz = src['gates']
    w2 = jnp.concatenate(cols + [w_in[:, a:a + sz]], axis=1)
    h2d = hcat.reshape(M, D).astype(bf16)
    px = _mm_w(h2d, w2, 0, gate_off, out_dtype=bf16).reshape(B, Lt, gate_off)

    def part(nm):
        a_, sz_ = dst[nm]
        return px[..., a_:a_ + sz_].astype(f32)

    na_k, na_v, na_q = part('na_k'), part('na_v'), part('na_q')
    rw_k, rw_v, rw_r = part('rw_k'), part('rw_v'), part('rw_r')
    rw_wl, rw_al, rw_gl = part('rw_wl'), part('rw_al'), part('rw_gl')
    ml_k, ml_v, ml_q, ml_o, ml_g = part('ml_k'), part('ml_v'), part('ml_q'), part('ml_o'), part('ml_g')
    fo = part('fo')

    C = BW // FOURIER_GROUPS
    cc, sc = _dft_mats(C, C ** -0.5)
    eye_g = jnp.eye(FOURIER_GROUPS, dtype=f32)
    chan = jnp.concatenate([jnp.kron(eye_g, cc), -jnp.kron(eye_g, sc)], axis=1)
    uc = _mm(fo.reshape(M, BW), chan).reshape(B, Lt, 2 * BW)

    def pos_dft(u, L):
        cl, sl_ = _dft_mats(L, L ** -0.5)
        rhs = jnp.concatenate([jnp.swapaxes(u[..., :BW], 0, 1).reshape(L, B * BW),
                               jnp.swapaxes(u[..., BW:], 0, 1).reshape(L, B * BW)], axis=0)
        y = _mm(jnp.concatenate([cl, sl_], axis=1), rhs, out_dtype=bf16)
        return jnp.swapaxes(y.reshape(L, B, BW), 0, 1)

    y_four = jnp.concatenate([pos_dft(uc[:, :Lc], Lc), pos_dft(uc[:, Lc:], S)], axis=1)

    n_rows = S // GRID_W
    win_r = min(NA_WIN_ROWS, n_rows)
    scale = HEAD_DIM ** -0.5
    t = jnp.arange(S)
    q_lat = na_q[:, Lc:].reshape(B, S, H, HEAD_DIM)
    k_lat = na_k[:, Lc:].reshape(B, S, H, HEAD_DIM)
    q_rot = (_axial_rope(q_lat, t // GRID_W, t % GRID_W) * scale).reshape(B, S, BW).astype(bf16)
    k_rot = _axial_rope(k_lat, t // GRID_W, t % GRID_W).reshape(B, S, BW).astype(bf16)
    q_plain = (na_q[:, Lc:] * scale).astype(bf16)
    kc = na_k[:, :Lc].astype(bf16)
    vc = na_v[:, :Lc].astype(bf16)
    bias_tab = _na_bias_table(lp['na_rpb'], win_r)
    y_na_x = _na_attention(q_rot, q_plain, k_rot, na_v[:, Lc:].astype(bf16), kc, vc, bias_tab, n_rows, win_r)
    if last:
        y_na_c = jnp.zeros((B, Lc, BW), bf16)
    else:
        y_na_c = _ctx_attention((na_q[:, :Lc] * scale).astype(bf16), kc, vc).astype(bf16)
    y_na = jnp.concatenate([y_na_c, y_na_x], axis=1)

    seqs = []
    bonus_in = []
    for d in (0, 1):
        mu = lp['rw_mu_rkv'][d]
        mu_wa = lp['rw_mu_wa'][d]

        def lerp(u, m_):
            return u + m_ * (_seg_map(lambda s: _shift(s, d), u, Lc) - u)

        r_l = lerp(rw_r, mu[:BW])
        k_l = lerp(rw_k, mu[BW:2 * BW])
        v_l = lerp(rw_v, mu[2 * BW:])
        wl_l = lerp(rw_wl[..., d * RW_DECAY_LORA:(d + 1) * RW_DECAY_LORA], mu_wa[:RW_DECAY_LORA])
        al_l = lerp(rw_al[..., d * RW_ICL_LORA:(d + 1) * RW_ICL_LORA], mu_wa[RW_DECAY_LORA:])
        w_pre = lp['rw_w0'][d] + _mm_w(jnp.tanh(wl_l).reshape(M, -1), lp['rw_w_up'][d]).reshape(B, Lt, BW)
        w_log = -jax.nn.softplus(-w_pre) - 0.5
        lw = -jnp.exp(w_log)
        a_icl = jax.nn.sigmoid(lp['rw_a0'][d] + _mm_w(al_l.reshape(M, -1), lp['rw_a_up'][d]).reshape(B, Lt, BW))
        kk = (k_l * lp['rw_k_k']).reshape(B, Lt, H, HEAD_DIM)
        kk = (kk / jnp.maximum(jnp.sqrt(jnp.sum(kk * kk, -1, keepdims=True)), 1e-12)).reshape(B, Lt, BW)
        k_mod = k_l * (1.0 + (a_icl - 1.0) * lp['rw_k_a'])
        seqs.append((lw, k_mod, v_l, -kk, kk * a_icl, r_l))
        bonus_in.append((r_l, k_mod, v_l))
    ys = _rwkv_scan(seqs[0], seqs[1], n_ctx=Lc // RW_CHUNK)
    y_rw = 0.0
    for d in (0, 1):
        y_d = _group_norm(ys[d], lp['rw_gn_g'][d], lp['rw_gn_b'][d], H, RW_GN_EPS)
        r_l, k_mod, v_l = bonus_in[d]
        hs = (B, Lt, H, HEAD_DIM)
        bonus = jnp.sum(r_l.reshape(hs) * k_mod.reshape(hs) * lp['rw_r_k'][d], -1, keepdims=True) * v_l.reshape(hs)
        y_rw = y_rw + y_d + bonus.reshape(B, Lt, BW)
    y_rw = y_rw * _mm_w(jax.nn.sigmoid(rw_gl).reshape(M, -1), lp['rw_g_up']).reshape(B, Lt, BW)

    k_m = jax.nn.silu(_seg_map(lambda s: _conv3(s, lp['ml_conv_k']), ml_k, Lc)) * scale
    q_m = jax.nn.silu(_seg_map(lambda s: _conv3(s, lp['ml_conv_q']), ml_q, Lc))
    g4 = ml_g.reshape(B, Lt, 4, H) + lp['ml_gate_b']
    li = g4[:, :, :2]
    lf = jax.nn.log_sigmoid(g4[:, :, 2:])
    li_s = jnp.concatenate([li[:, :, 0], li[:, :, 1]], axis=0)
    lf_s = jnp.concatenate([lf[:, :, 0], lf[:, :, 1]], axis=0)
    hs_ = _mlstm_scan(q_m, k_m, ml_v, li_s, lf_s, n_ctx=Lc // ML_CHUNK)
    h_sum = hs_[0] + hs_[1]
    y_ml = _group_norm(h_sum, lp['ml_gn_g'], None, H, GN_EPS) * jax.nn.sigmoid(ml_o)

    ys4 = [u.reshape(M, BW).astype(bf16) for u in (y_four, y_na, y_rw, y_ml)]
    acc = _merge(h2d, ys4, w2, gate_off, lp['w_branch'], lp['layer'])
    return _mm_w(acc, lp['w_out'], lead=(lp['layer'],)).reshape(B, Lt, D)


def kernel(x, c, ctx, c_ctx, w_ada, b_ada, g_norm1, g_norm2, w_in, na_rpb, rw_mu_rkv, rw_mu_wa, rw_w_up, rw_w0, rw_a_up, rw_a0, rw_g_up, rw_k_k, rw_k_a, rw_r_k, rw_gn_g, rw_gn_b, ml_conv_q, ml_conv_k, ml_gate_b, ml_gn_g, w_branch, w_out, w_router, b_router, w_gu, b_gu, w_dn, b_dn, g_final):
    B, S, D = x.shape
    Lc = ctx.shape[1]
    depth = w_ada.shape[0]
    mod_in = jnp.concatenate([jax.nn.silu(c), jax.nn.silu(c_ctx)[None]], axis=0)
    mod_in = jnp.pad(mod_in, ((0, (-(B + 1)) % 16), (0, 0)))
    Lt = Lc + S
    pending = None
    for l in range(depth):
        last = l == depth - 1
        lp = {'w_in': w_in[l], 'na_rpb': na_rpb[l], 'rw_mu_rkv': rw_mu_rkv[l], 'rw_mu_wa': rw_mu_wa[l],
              'rw_w_up': rw_w_up[l], 'rw_w0': rw_w0[l], 'rw_a_up': rw_a_up[l], 'rw_a0': rw_a0[l],
              'rw_g_up': rw_g_up[l], 'rw_k_k': rw_k_k[l], 'rw_k_a': rw_k_a[l], 'rw_r_k': rw_r_k[l],
              'rw_gn_g': rw_gn_g[l], 'rw_gn_b': rw_gn_b[l], 'ml_conv_q': ml_conv_q[l],
              'ml_conv_k': ml_conv_k[l], 'ml_gate_b': ml_gate_b[l], 'ml_gn_g': ml_gn_g[l],
              'w_branch': w_branch, 'w_out': w_out, 'layer': l}
        mod = _mm_w(mod_in, w_ada, lead=(l,)) + b_ada[l]
        mx = jnp.split(mod[:B], 6, axis=-1)
        mc = jnp.split(mod[B], 6, axis=-1)
        if pending is None:
            hx, = _norm_mod(x, g_norm1[l], mx[1], mx[0], lo=False)
        else:
            x, hx = _norm_mod(x, g_norm1[l], mx[1], mx[0], res=pending, lo=False)
        hc = _rms_norm(ctx, g_norm1[l]) * (1.0 + mc[1]) + mc[0]
        y = _mixer(jnp.concatenate([hc.astype(bf16), hx], axis=1), lp, B, Lc, S, last)
        res = (y.reshape(B * Lt, D), Lc, Lt, mx[2])
        if last:
            x, hx2_hi, hx2_lo = _norm_mod(x, g_norm2[l], mx[4], mx[3], res=res)
            ym = _moe(hx2_hi.reshape(-1, D), hx2_lo.reshape(-1, D), w_router[l], b_router[l],
                      w_gu, b_gu, w_dn, b_dn, l)
        else:
            x, hx2 = _norm_mod(x, g_norm2[l], mx[4], mx[3], res=res, split=False)
            ctx = ctx + mc[2] * y[:, :Lc]
            hc2 = _rms_norm(ctx, g_norm2[l]) * (1.0 + mc[4]) + mc[3]
            h_all = jnp.concatenate([hx2.reshape(-1, D), hc2.reshape(-1, D)], axis=0)
            h_hi = h_all.astype(bf16)
            n_x = B * S
            ym = _moe(h_hi, (h_all - h_hi.astype(f32)).astype(bf16),
                      w_router[l], b_router[l], w_gu, b_gu, w_dn, b_dn, l)
            ctx = ctx + mc[5] * ym[n_x:].reshape(ctx.shape)
        pending = (ym, 0, S, mx[5])
    zero = jnp.zeros((B, D), f32)
    return _norm_mod(x, g_final, zero, zero, res=pending, emit_x=False, split=False)[0]
```
